```python
import math
import jax
import jax.numpy as jnp
from jax import lax
import numpy as np


D_MODEL = 1024
BATCH = 1
SEQ = 16384
DEPTH = 1
DEC_BATCH = 8
DEC_SEQ = 8192
PAST_LEN = 128

N_MEM = 256
GRID_W = 64
EPS = 1e-6
SSM_HEADS = 16
SSM_HEAD_DIM = 64
SSM_INNER = SSM_HEADS * SSM_HEAD_DIM
SSM_GROUPS = 2
SSM_STATE = 128
D_CONV = 5
CHUNK = 128
CONV_DIM = SSM_INNER + 2 * SSM_GROUPS * SSM_STATE
DT_MIN = 1e-3
DT_MAX = 1e-1
ATTN_HEADS = 8
ATTN_KV_HEADS = 2
ATTN_HEAD_DIM = 64
ATTN_INNER = ATTN_HEADS * ATTN_HEAD_DIM
ROPE_THETA = 10000.0
Q_BLOCK = 128
MIX_WIDTH = SSM_INNER + ATTN_INNER
IN_SPLITS = (SSM_INNER, CONV_DIM, 2 * SSM_HEADS, ATTN_INNER,
             ATTN_KV_HEADS * ATTN_HEAD_DIM, ATTN_KV_HEADS * ATTN_HEAD_DIM)
D_IN_PROJ = sum(IN_SPLITS)
IN_SPLIT_IDX = [int(v) for v in np.cumsum(IN_SPLITS)[:-1]]
X_HEADS = 4
X_HEAD_DIM = 128
X_INNER = X_HEADS * X_HEAD_DIM
D_FF = -(-8 * D_MODEL // (3 * 256)) * 256

kernel_name = 'hymba_ssd_axial_gqa_sandwich_encoder'


def rmsnorm(x, g):
    xf = x.astype(jnp.float32)
    y = xf * lax.rsqrt(jnp.mean(xf * xf, axis=-1, keepdims=True) + EPS)
    return (y * g.astype(jnp.float32)).astype(x.dtype)


def segsum(a):
    cs = jnp.cumsum(a, axis=-1)
    t = a.shape[-1]
    diff = cs[..., :, None] - cs[..., None, :]
    mask = jnp.tril(jnp.ones((t, t), dtype=bool))
    return jnp.where(mask, diff, -jnp.inf)


def ssd_scan(xh, dt, a, bmat, cmat):
    b, L, H, P = xh.shape
    c = L // CHUNK
    e = H // SSM_GROUPS
    f32 = jnp.float32
    X = (xh.astype(f32) * dt[..., None]).reshape(b, c, CHUNK, SSM_GROUPS, e, P)
    A = (dt * a).reshape(b, c, CHUNK, SSM_GROUPS, e).transpose(0, 3, 4, 1, 2)
    Bc = bmat.astype(f32).reshape(b, c, CHUNK, SSM_GROUPS, SSM_STATE)
    Cc = cmat.astype(f32).reshape(b, c, CHUNK, SSM_GROUPS, SSM_STATE)
    A_cs = jnp.cumsum(A, axis=-1)
    Lmat = jnp.exp(segsum(A))
    CB = jnp.einsum('bclgn,bcsgn->bcgls', Cc, Bc)
    y_diag = jnp.einsum('bcgls,bgecls,bcsgep->bclgep', CB, Lmat, X)
    decay_states = jnp.exp(A_cs[..., -1:] - A_cs)
    states = jnp.einsum('bcsgn,bgecs,bcsgep->bcgepn', Bc, decay_states, X)
    states = jnp.concatenate([jnp.zeros_like(states[:, :1]), states], axis=1)
    chunk_decay = jnp.exp(segsum(jnp.pad(A_cs[..., -1], ((0, 0), (0, 0), (0, 0), (1, 0)))))
    states = jnp.einsum('bgezc,bcgepn->bzgepn', chunk_decay, states)[:, :-1]
    y_off = jnp.einsum('bclgn,bcgepn,bgecl->bclgep', Cc, states, jnp.exp(A_cs))
    return (y_diag + y_off).reshape(b, L, H, P)


def dwconv_centred(x, w, bias):
    out = lax.conv_general_dilated(
        x, w[:, None, :].astype(x.dtype), window_strides=(1,),
        padding=[(D_CONV // 2, D_CONV // 2)],
        dimension_numbers=('NWC', 'WIO', 'NWC'),
        feature_group_count=x.shape[-1])
    return out + bias.astype(x.dtype)


def ssd_mixer(xbc_raw, z, dt_raw, conv_w, conv_b, dt_bias, a_log, d_skip, ssm_norm):
    b, L, _ = xbc_raw.shape
    f32 = jnp.float32
    xbc = jax.nn.silu(dwconv_centred(xbc_raw, conv_w, conv_b))
    xs, bm, cm = jnp.split(xbc, [SSM_INNER, SSM_INNER + SSM_GROUPS * SSM_STATE], axis=-1)
    xh = xs.reshape(b, L, SSM_HEADS, SSM_HEAD_DIM)
    bm = bm.reshape(b, L, SSM_GROUPS, SSM_STATE)
    cm = cm.reshape(b, L, SSM_GROUPS, SSM_STATE)
    dt = jax.nn.softplus(dt_raw.astype(f32).reshape(b, L, 2, SSM_HEADS) + dt_bias.astype(f32))
    a = -jnp.exp(a_log.astype(f32))
    flip = lambda t: jnp.flip(t, axis=1)
    y_f = ssd_scan(xh, dt[:, :, 0], a[0], bm, cm)
    y_b = flip(ssd_scan(flip(xh), flip(dt[:, :, 1]), a[1], flip(bm), flip(cm)))
    y = y_f + y_b + xh.astype(f32) * d_skip.astype(f32)[:, None]
    y = y.reshape(b, L, SSM_INNER) * jax.nn.silu(z.astype(f32))
    yg = y.reshape(b, L, SSM_GROUPS, SSM_INNER // SSM_GROUPS)
    yg = yg * lax.rsqrt(jnp.mean(yg * yg, axis=-1, keepdims=True) + EPS)
    y = yg.reshape(b, L, SSM_INNER) * ssm_norm.astype(f32)
    return y.astype(xbc_raw.dtype)


def axial_angles(L):
    rows = L // GRID_W
    r = jnp.repeat(jnp.arange(rows, dtype=jnp.float32), GRID_W)
    c = jnp.tile(jnp.arange(GRID_W, dtype=jnp.float32), rows)
    half = ATTN_HEAD_DIM // 2
    inv = 1.0 / (ROPE_THETA ** (jnp.arange(0, half, 2, dtype=jnp.float32) / half))
    return r[:, None] * inv, c[:, None] * inv


def rope1d(x, ang):
    cos = jnp.cos(ang)[None, :, None, :]
    sin = jnp.sin(ang)[None, :, None, :]
    x1, x2 = jnp.split(x, 2, axis=-1)
    return jnp.concatenate([x1 * cos - x2 * sin, x2 * cos + x1 * sin], axis=-1)


def axial_rope(x, ang_r, ang_c):
    xf = x.astype(jnp.float32)
    half = ATTN_HEAD_DIM // 2
    out = jnp.concatenate([rope1d(xf[..., :half], ang_r), rope1d(xf[..., half:], ang_c)], axis=-1)
    return out.astype(x.dtype)


def self_attention(q, k, v, q_norm, k_norm):
    b, L, _ = q.shape
    g = ATTN_HEADS // ATTN_KV_HEADS
    q = rmsnorm(q.reshape(b, L, ATTN_HEADS, ATTN_HEAD_DIM), q_norm)
    k = rmsnorm(k.reshape(b, L, ATTN_KV_HEADS, ATTN_HEAD_DIM), k_norm)
    v = v.reshape(b, L, ATTN_KV_HEADS, ATTN_HEAD_DIM)
    ang_r, ang_c = axial_angles(L)
    q = axial_rope(q, ang_r, ang_c)
    k = axial_rope(k, ang_r, ang_c)
    nblk = L // Q_BLOCK
    qb = q.reshape(b, nblk, Q_BLOCK, ATTN_KV_HEADS, g, ATTN_HEAD_DIM).transpose(1, 0, 2, 3, 4, 5)
    scale = ATTN_HEAD_DIM ** -0.5

    def block(qi):
        s = jnp.einsum('bqkgd,bskd->bkgqs', qi, k).astype(jnp.float32) * scale
        p = jax.nn.softmax(s, axis=-1).astype(v.dtype)
        return jnp.einsum('bkgqs,bskd->bqkgd', p, v)

    o = lax.map(block, qb)
    return o.transpose(1, 0, 2, 3, 4, 5).reshape(b, L, ATTN_INNER)


def cross_attention(h, mem_n, wq, wk, wv, wo):
    b, L, _ = h.shape
    m = mem_n.shape[1]
    q = (h @ wq).reshape(b, L, X_HEADS, X_HEAD_DIM)
    k = (mem_n @ wk).reshape(b, m, X_HEADS, X_HEAD_DIM)
    v = (mem_n @ wv).reshape(b, m, X_HEADS, X_HEAD_DIM)
    s = jnp.einsum('blhd,bmhd->bhlm', q, k).astype(jnp.float32) * (X_HEAD_DIM ** -0.5)
    p = jax.nn.softmax(s, axis=-1).astype(v.dtype)
    o = jnp.einsum('bhlm,bmhd->blhd', p, v).reshape(b, L, X_INNER)
    return o @ wo


def encoder_layer(x, mem, p, i):
    h = rmsnorm(x, p['norm_mix_pre'][i])
    proj = h @ p['w_in'][i]
    z, xbc, dtr, q, k, v = jnp.split(proj, IN_SPLIT_IDX, axis=-1)
    y_ssm = ssd_mixer(xbc, z, dtr, p['conv_w'][i], p['conv_b'][i], p['dt_bias'][i],
                      p['a_log'][i], p['d_skip'][i], p['ssm_norm'][i])
    y_att = rmsnorm(self_attention(q, k, v, p['q_norm'][i], p['k_norm'][i]), p['attn_norm'][i])
    mix = jnp.concatenate([y_ssm, y_att], axis=-1) @ p['w_out'][i]
    x = x + rmsnorm(mix, p['norm_mix_post'][i])
    h = rmsnorm(x, p['norm_x_pre'][i])
    m = rmsnorm(mem, p['norm_mem'][i])
    ca = cross_attention(h, m, p['w_xq'][i], p['w_xk'][i], p['w_xv'][i], p['w_xo'][i])
    x = x + rmsnorm(ca, p['norm_x_post'][i])
    h = rmsnorm(x, p['norm_ffn_pre'][i])
    f = (jax.nn.silu(h @ p['w_gate'][i]) * (h @ p['w_up'][i])) @ p['w_down'][i]
    x = x + rmsnorm(f, p['norm_ffn_post'][i])
    return x


def setup_inputs(seed: int = 0) -> dict:
    key = jax.random.key(seed)
    ks = jax.random.split(key, 32)
    f32 = jnp.float32

    def nrm(k, shape, fan_in):
        return jax.random.normal(k, shape, f32) * (fan_in ** -0.5)

    def gain(k, shape):
        return 1.0 + 0.02 * jax.random.normal(k, shape, f32)

    u = jax.random.uniform(ks[8], (DEPTH, 2, SSM_HEADS), f32)
    dt0 = jnp.exp(u * (math.log(DT_MAX) - math.log(DT_MIN)) + math.log(DT_MIN))
    dt_bias = dt0 + jnp.log(-jnp.expm1(-dt0))
    a_log = jnp.log(jax.random.uniform(ks[9], (DEPTH, 2, SSM_HEADS), f32, 1.0, 16.0))
    return {
        'x_prompt': jax.random.normal(ks[0], (BATCH, SEQ, D_MODEL), f32),
        'x_sample': jax.random.normal(ks[1], (DEC_BATCH, DEC_SEQ, D_MODEL), f32),
        'mem_prompt': jax.random.normal(ks[2], (BATCH, N_MEM, D_MODEL), f32),
        'mem_sample': jax.random.normal(ks[3], (DEC_BATCH, N_MEM, D_MODEL), f32),
        'norm_mix_pre': gain(ks[4], (DEPTH, D_MODEL)),
        'w_in': nrm(ks[5], (DEPTH, D_MODEL, D_IN_PROJ), D_MODEL),
        'conv_w': nrm(ks[6], (DEPTH, D_CONV, CONV_DIM), D_CONV),
        'conv_b': 0.01 * jax.random.normal(ks[7], (DEPTH, CONV_DIM), f32),
        'dt_bias': dt_bias,
        'a_log': a_log,
        'd_skip': gain(ks[10], (DEPTH, SSM_HEADS)),
        'ssm_norm': gain(ks[11], (DEPTH, SSM_INNER)),
        'q_norm': gain(ks[12], (DEPTH, ATTN_HEAD_DIM)),
        'k_norm': gain(ks[13], (DEPTH, ATTN_HEAD_DIM)),
        'attn_norm': gain(ks[14], (DEPTH, ATTN_INNER)),
        'w_out': nrm(ks[15], (DEPTH, MIX_WIDTH, D_MODEL), MIX_WIDTH),
        'norm_mix_post': gain(ks[16], (DEPTH, D_MODEL)),
        'norm_x_pre': gain(ks[17], (DEPTH, D_MODEL)),
        'norm_mem': gain(ks[18], (DEPTH, D_MODEL)),
        'w_xq': nrm(ks[19], (DEPTH, D_MODEL, X_INNER), D_MODEL),
        'w_xk': nrm(ks[20], (DEPTH, D_MODEL, X_INNER), D_MODEL),
        'w_xv': nrm(ks[21], (DEPTH, D_MODEL, X_INNER), D_MODEL),
        'w_xo': nrm(ks[22], (DEPTH, X_INNER, D_MODEL), X_INNER),
        'norm_x_post': gain(ks[23], (DEPTH, D_MODEL)),
        'norm_ffn_pre': gain(ks[24], (DEPTH, D_MODEL)),
        'w_gate': nrm(ks[25], (DEPTH, D_MODEL, D_FF), D_MODEL),
        'w_up': nrm(ks[26], (DEPTH, D_MODEL, D_FF), D_MODEL),
        'w_down': nrm(ks[27], (DEPTH, D_FF, D_MODEL), D_FF),
        'norm_ffn_post': gain(ks[28], (DEPTH, D_MODEL)),
    }


def reference(x_prompt, x_sample, mem_prompt, mem_sample, norm_mix_pre, w_in, conv_w, conv_b,
              dt_bias, a_log, d_skip, ssm_norm, q_norm, k_norm, attn_norm, w_out, norm_mix_post,
              norm_x_pre, norm_mem, w_xq, w_xk, w_xv, w_xo, norm_x_post, norm_ffn_pre,
              w_gate, w_up, w_down, norm_ffn_post):
    p = {
        'norm_mix_pre': norm_mix_pre, 'w_in': w_in, 'conv_w': conv_w, 'conv_b': conv_b,
        'dt_bias': dt_bias, 'a_log': a_log, 'd_skip': d_skip, 'ssm_norm': ssm_norm,
        'q_norm': q_norm, 'k_norm': k_norm, 'attn_norm': attn_norm, 'w_out': w_out,
        'norm_mix_post': norm_mix_post, 'norm_x_pre': norm_x_pre, 'norm_mem': norm_mem,
        'w_xq': w_xq, 'w_xk': w_xk, 'w_xv': w_xv, 'w_xo': w_xo, 'norm_x_post': norm_x_post,
        'norm_ffn_pre': norm_ffn_pre, 'w_gate': w_gate, 'w_up': w_up, 'w_down': w_down,
        'norm_ffn_post': norm_ffn_post,
    }
    y_prompt = x_prompt
    y_sample = x_sample
    for i in range(DEPTH):
        y_prompt = encoder_layer(y_prompt, mem_prompt, p, i)
        y_sample = encoder_layer(y_sample, mem_sample, p, i)
    return (y_prompt, y_sample)
```

```python
import functools

import numpy as np
import jax
import jax.numpy as jnp
from jax import lax
from jax.experimental import pallas as pl
from jax.experimental.pallas import tpu as pltpu

F32 = jnp.float32
BF16 = jnp.bfloat16

EPS = 1e-6
LANES = 128
SUBLANES = 8
VMEM_LIMIT_BYTES = 56 * 1024 * 1024

D_MODEL = 1024
SSM_HEADS = 16
SSM_HEAD_DIM = 64
SSM_INNER = SSM_HEADS * SSM_HEAD_DIM
SSM_GROUPS = 2
SSM_STATE = 128
D_CONV = 5
CHUNK = 128
CONV_DIM = SSM_INNER + 2 * SSM_GROUPS * SSM_STATE
ATTN_HEADS = 8
ATTN_KV_HEADS = 2
ATTN_HEAD_DIM = 64
ATTN_INNER = ATTN_HEADS * ATTN_HEAD_DIM
ATTN_GROUP = ATTN_HEADS // ATTN_KV_HEADS
GRID_W = 64
ROPE_THETA = 10000.0
X_HEADS = 4
X_HEAD_DIM = 128
X_INNER = X_HEADS * X_HEAD_DIM
DT_PAD = LANES
HALO = SUBLANES


def _dot(a, b):
    return jnp.dot(a, b, preferred_element_type=F32)


def _dot_nt(a, b):
    return lax.dot_general(a, b, (((1,), (1,)), ((), ())), preferred_element_type=F32)


def _dot_tn(a, b):
    return lax.dot_general(a, b, (((0,), (0,)), ((), ())), preferred_element_type=F32)


def _rmsnorm(x, g):
    return x * lax.rsqrt(jnp.mean(x * x, axis=-1, keepdims=True) + EPS) * g


def _sigmoid(x):
    return 1.0 / (1.0 + jnp.exp(-x))


def _params(*sem):
    return pltpu.CompilerParams(dimension_semantics=sem, vmem_limit_bytes=VMEM_LIMIT_BYTES)


def _const_spec(shape):
    nd = len(shape)
    return pl.BlockSpec(shape, lambda *_: (0,) * nd)


def _in_proj_kernel(x_ref, g_ref, wz_ref, wxbc_ref, wdt_ref, wq_ref, wk_ref, wv_ref,
                    cos_ref, sin_ref, qg_ref, kg_ref,
                    z_ref, xbc_ref, dt_ref, qp_ref, kp_ref, vt_ref):
    x = x_ref[0]
    h = _rmsnorm(x, g_ref[...]).astype(BF16)
    z_ref[0] = _dot(h, wz_ref[...])
    xbc_ref[0] = _dot(h, wxbc_ref[...])
    dt_ref[0] = _dot(h, wdt_ref[...])
    q = _dot(h, wq_ref[...])
    k = _dot(h, wk_ref[...])
    v = _dot(h, wv_ref[...])
    vt_ref[0] = v.T.astype(BF16)

    cos = cos_ref[...]
    sin = sin_ref[...]
    lane = lax.broadcasted_iota(jnp.int32, cos.shape, 1)

    def head_norm_rope(blk, gain):
        ss = blk * blk
        for s in (32, 16, 8, 4, 2, 1):
            ss = ss + jnp.where((lane & s) != 0, pltpu.roll(ss, s, 1), pltpu.roll(ss, LANES - s, 1))
        y = blk * lax.rsqrt(ss * (1.0 / ATTN_HEAD_DIM) + EPS) * gain
        partner = jnp.where((lane & 16) != 0, pltpu.roll(y, 16, 1), pltpu.roll(y, LANES - 16, 1))
        return y * cos + partner * sin

    kp_ref[0] = head_norm_rope(k, kg_ref[...]).astype(BF16)
    low = lane < ATTN_HEAD_DIM
    qg = qg_ref[...]
    for j in range(ATTN_GROUP):
        r = head_norm_rope(q[:, j * LANES:(j + 1) * LANES], qg)
        qp_ref[0, :, j * LANES:(j + 1) * LANES] = jnp.where(low, r, 0.0).astype(BF16)
        qp_ref[0, :, (ATTN_GROUP + j) * LANES:(ATTN_GROUP + j + 1) * LANES] = jnp.where(low, 0.0, r).astype(BF16)


def _in_proj(x, g, wz, wxbc, wdt, wq, wk, wv, cos, sin, qg, kg, tm):
    b, l, d = x.shape
    grid = (b, l // tm)
    tok = lambda n: pl.BlockSpec((1, tm, n), lambda bi, i: (bi, i, 0))
    pos = pl.BlockSpec((tm, LANES), lambda bi, i: (i, 0))
    out_shape = (
        jax.ShapeDtypeStruct((b, l, SSM_INNER), F32),
        jax.ShapeDtypeStruct((b, l, CONV_DIM), F32),
        jax.ShapeDtypeStruct((b, l, DT_PAD), F32),
        jax.ShapeDtypeStruct((b, l, ATTN_HEADS * LANES), BF16),
        jax.ShapeDtypeStruct((b, l, LANES), BF16),
        jax.ShapeDtypeStruct((b, LANES, l), BF16),
    )
    return pl.pallas_call(
        _in_proj_kernel,
        grid=grid,
        in_specs=[tok(d), _const_spec(g.shape), _const_spec(wz.shape), _const_spec(wxbc.shape),
                  _const_spec(wdt.shape), _const_spec(wq.shape), _const_spec(wk.shape),
                  _const_spec(wv.shape), pos, pos, _const_spec(qg.shape), _const_spec(kg.shape)],
        out_specs=(tok(SSM_INNER), tok(CONV_DIM), tok(DT_PAD), tok(ATTN_HEADS * LANES), tok(LANES),
                   pl.BlockSpec((1, LANES, tm), lambda bi, i: (bi, 0, i))),
        out_shape=out_shape,
        compiler_params=_params("parallel", "parallel"),
        name="in_proj",
    )(x, g, wz, wxbc, wdt, wq, wk, wv, cos, sin, qg, kg)


def _conv_kernel(xm_ref, xp_ref, xn_ref, w_ref, b_ref, o_ref, ext_ref, *, tc):
    i = pl.program_id(1)
    n = pl.num_programs(1)
    ext_ref[0:HALO, :] = jnp.where(i > 0, xp_ref[0], 0.0)
    ext_ref[HALO:HALO + tc, :] = xm_ref[0]
    ext_ref[HALO + tc:HALO + tc + HALO, :] = jnp.where(i < n - 1, xn_ref[0], 0.0)
    acc = jnp.broadcast_to(b_ref[...], (tc, CONV_DIM))
    for j in range(D_CONV):
        off = HALO - D_CONV // 2 + j
        acc = acc + ext_ref[off:off + tc, :] * w_ref[j:j + 1, :]
    o_ref[0] = (acc * _sigmoid(acc)).astype(BF16)


def _conv(xbc, w, bias, tc):
    b, l, c = xbc.shape
    nh = tc // HALO
    last = l // HALO - 1
    return pl.pallas_call(
        functools.partial(_conv_kernel, tc=tc),
        grid=(b, l // tc),
        in_specs=[pl.BlockSpec((1, tc, c), lambda bi, i: (bi, i, 0)),
                  pl.BlockSpec((1, HALO, c), lambda bi, i: (bi, jnp.maximum(i * nh - 1, 0), 0)),
                  pl.BlockSpec((1, HALO, c), lambda bi, i: (bi, jnp.minimum((i + 1) * nh, last), 0)),
                  _const_spec(w.shape), _const_spec(bias.shape)],
        out_specs=pl.BlockSpec((1, tc, c), lambda bi, i: (bi, i, 0)),
        out_shape=jax.ShapeDtypeStruct((b, l, c), BF16),
        scratch_shapes=[pltpu.VMEM((tc + 2 * HALO, c), F32)],
        compiler_params=_params("parallel", "parallel"),
        name="conv",
    )(xbc, xbc, xbc, w, bias)


def _expand_heads(w, off):
    lane = lax.broadcasted_iota(jnp.int32, (CHUNK, LANES), 1)
    low = lane < SSM_HEAD_DIM
    tiles = []
    for j in range(SSM_HEADS // 2):
        a = jnp.broadcast_to(w[:, off + 2 * j:off + 2 * j + 1], (CHUNK, LANES))
        b = jnp.broadcast_to(w[:, off + 2 * j + 1:off + 2 * j + 2], (CHUNK, LANES))
        tiles.append(jnp.where(low, a, b))
    return jnp.concatenate(tiles, axis=1)


def _ssd_direction(xc, dt_raw, dtb, a_neg, h_ref, reverse):
    off = SSM_HEADS if reverse else 0
    row = lax.broadcasted_iota(jnp.int32, (CHUNK, CHUNK), 0)
    col = lax.broadcasted_iota(jnp.int32, (CHUNK, CHUNK), 1)
    mask = (row <= col) if reverse else (row >= col)
    tri = mask.astype(F32)
    xpre = dt_raw + dtb
    dtv = jnp.maximum(xpre, 0.0) + jnp.log1p(jnp.exp(-jnp.abs(xpre)))
    a = dtv * a_neg
    cs = jnp.dot(tri, a, preferred_element_type=F32, precision=lax.Precision.HIGHEST)
    end = 0 if reverse else CHUNK - 1
    cs_end = cs[end:end + 1, :]
    e_cs = jnp.exp(cs)
    w_state = dtv * jnp.exp(cs_end - cs)
    cs_t = cs.T
    dt_t = dtv.T
    e3 = _expand_heads(e_cs, off)
    e2 = _expand_heads(w_state, off)
    xs = xc[:, :SSM_INNER]
    xdec = (xs.astype(F32) * e2).astype(BF16)
    lane = lax.broadcasted_iota(jnp.int32, (CHUNK, LANES), 1)
    low = lane < SSM_HEAD_DIM
    zero = jnp.zeros((CHUNK, LANES), BF16)
    gw = SSM_INNER // SSM_GROUPS
    hpg = SSM_HEADS // SSM_GROUPS
    y_tiles = []
    for g in range(SSM_GROUPS):
        bm = xc[:, SSM_INNER + g * SSM_STATE:SSM_INNER + (g + 1) * SSM_STATE]
        cm = xc[:, SSM_INNER + (SSM_GROUPS + g) * SSM_STATE:SSM_INNER + (SSM_GROUPS + g + 1) * SSM_STATE]
        cb = _dot_nt(cm, bm)
        h_prev = h_ref[:, g * gw:(g + 1) * gw]
        y_off = _dot(cm, h_prev.astype(BF16)) * e3[:, g * gw:(g + 1) * gw]
        for jj in range(hpg // 2):
            j = g * (hpg // 2) + jj
            x_tile = xs[:, j * LANES:(j + 1) * LANES]
            acc = y_off[:, jj * LANES:(jj + 1) * LANES]
            for half in range(2):
                hd = off + 2 * j + half
                seg = cs[:, hd:hd + 1] - cs_t[hd:hd + 1, :]
                m = cb * jnp.exp(jnp.where(mask, seg, -jnp.inf)) * dt_t[hd:hd + 1, :]
                x_half = jnp.where(low, x_tile, zero) if half == 0 else jnp.where(low, zero, x_tile)
                acc = acc + _dot(m.astype(BF16), x_half)
            y_tiles.append(acc)
        s_new = _dot_tn(bm, xdec[:, g * gw:(g + 1) * gw])
        h_ref[:, g * gw:(g + 1) * gw] = h_prev * e3[end:end + 1, g * gw:(g + 1) * gw] + s_new
    return jnp.concatenate(y_tiles, axis=1)


def _ssd_kernel(xf_ref, xb_ref, dtf_ref, dtb_ref, bias_ref, alog_ref, dskip_ref,
                yf_ref, yb_ref, hf_ref, hb_ref):
    c = pl.program_id(1)

    @pl.when(c == 0)
    def _():
        hf_ref[...] = jnp.zeros_like(hf_ref)
        hb_ref[...] = jnp.zeros_like(hb_ref)

    a_neg = -jnp.exp(alog_ref[...])
    bias = bias_ref[...]
    xf = xf_ref[0]
    yf = _ssd_direction(xf, dtf_ref[0], bias, a_neg, hf_ref, reverse=False)
    yf_ref[0] = yf + xf[:, :SSM_INNER].astype(F32) * dskip_ref[...]
    yb_ref[0] = _ssd_direction(xb_ref[0], dtb_ref[0], bias, a_neg, hb_ref, reverse=True)


def _ssd(xc, dt, bias, alog, dskip):
    b, l, _ = xc.shape
    nc = l // CHUNK
    fwd = lambda n: pl.BlockSpec((1, CHUNK, n), lambda bi, c: (bi, c, 0))
    bwd = lambda n: pl.BlockSpec((1, CHUNK, n), lambda bi, c: (bi, nc - 1 - c, 0))
    return pl.pallas_call(
        _ssd_kernel,
        grid=(b, nc),
        in_specs=[fwd(CONV_DIM), bwd(CONV_DIM), fwd(DT_PAD), bwd(DT_PAD),
                  _const_spec(bias.shape), _const_spec(alog.shape), _const_spec(dskip.shape)],
        out_specs=(fwd(SSM_INNER), bwd(SSM_INNER)),
        out_shape=(jax.ShapeDtypeStruct((b, l, SSM_INNER), F32),
                   jax.ShapeDtypeStruct((b, l, SSM_INNER), F32)),
        scratch_shapes=[pltpu.VMEM((SSM_STATE, SSM_INNER), F32),
                        pltpu.VMEM((SSM_STATE, SSM_INNER), F32)],
        compiler_params=_params("parallel", "arbitrary"),
        name="ssd",
    )(xc, xc, dt, dt, bias, alog, dskip)


def _attn_kernel(q_ref, k_ref, vt_ref, o_ref, m_ref, l_ref, acc_ref, *, tk):
    nk = k_ref.shape[1] // tk
    m_ref[...] = jnp.full_like(m_ref, -jnp.inf)
    l_ref[...] = jnp.zeros_like(l_ref)
    acc_ref[...] = jnp.zeros_like(acc_ref)

    def body(i, carry):
        ks = pl.multiple_of(i * tk, tk)
        k = k_ref[0, pl.ds(ks, tk), :]
        vt = vt_ref[0, :, pl.ds(ks, tk)]
        for j in range(ATTN_GROUP):
            rows = slice(j * ATTN_HEAD_DIM, (j + 1) * ATTN_HEAD_DIM)
            s_t = _dot_nt(k, q_ref[0, :, j * LANES:(j + 1) * LANES])
            m_prev = m_ref[j:j + 1, :]
            m_new = jnp.maximum(m_prev, jnp.max(s_t, axis=0, keepdims=True))
            alpha = jnp.exp(m_prev - m_new)
            p_t = jnp.exp(s_t - m_new)
            l_ref[j:j + 1, :] = alpha * l_ref[j:j + 1, :] + jnp.sum(p_t, axis=0, keepdims=True)
            acc_ref[rows, :] = alpha * acc_ref[rows, :] + _dot(vt, p_t.astype(BF16))
            m_ref[j:j + 1, :] = m_new
        return carry

    lax.fori_loop(0, nk, body, 0)
    for j in range(ATTN_GROUP):
        rows = slice(j * ATTN_HEAD_DIM, (j + 1) * ATTN_HEAD_DIM)
        acc_ref[rows, :] = acc_ref[rows, :] / l_ref[j:j + 1, :]
    o_ref[0] = acc_ref[...].T


def _attention(qp, kp, vt, tq, tk):
    b, l, _ = qp.shape
    gw = ATTN_GROUP * LANES
    return pl.pallas_call(
        functools.partial(_attn_kernel, tk=tk),
        grid=(b, ATTN_KV_HEADS, l // tq),
        in_specs=[pl.BlockSpec((1, tq, gw), lambda bi, g, i: (bi, i, g)),
                  pl.BlockSpec((1, l, LANES), lambda bi, g, i: (bi, 0, 0)),
                  pl.BlockSpec((1, ATTN_HEAD_DIM, l), lambda bi, g, i: (bi, g, 0))],
        out_specs=pl.BlockSpec((1, tq, ATTN_GROUP * ATTN_HEAD_DIM), lambda bi, g, i: (bi, i, g)),
        out_shape=jax.ShapeDtypeStruct((b, l, ATTN_INNER), F32),
        scratch_shapes=[pltpu.VMEM((SUBLANES, tq), F32), pltpu.VMEM((SUBLANES, tq), F32),
                        pltpu.VMEM((ATTN_GROUP * ATTN_HEAD_DIM, tq), F32)],
        compiler_params=_params("parallel", "parallel", "parallel"),
        name="attention",
    )(qp, kp, vt)


def _mem_kv_kernel(mem_ref, g_ref, wk_ref, wv_ref, k_ref, v_ref):
    m = _rmsnorm(mem_ref[0], g_ref[...]).astype(BF16)
    k_ref[0] = _dot(m, wk_ref[...]).astype(BF16)
    v_ref[0] = _dot(m, wv_ref[...]).astype(BF16)


def _mem_kv(mem, g, wk, wv):
    b, n, d = mem.shape
    spec = pl.BlockSpec((1, n, X_INNER), lambda bi: (bi, 0, 0))
    return pl.pallas_call(
        _mem_kv_kernel,
        grid=(b,),
        in_specs=[pl.BlockSpec((1, n, d), lambda bi: (bi, 0, 0)), _const_spec(g.shape),
                  _const_spec(wk.shape), _const_spec(wv.shape)],
        out_specs=(spec, spec),
        out_shape=(jax.ShapeDtypeStruct((b, n, X_INNER), BF16),) * 2,
        compiler_params=_params("parallel"),
        name="mem_kv",
    )(mem, g, wk, wv)


def _mix_xattn_kernel(x_ref, yf_ref, yb_ref, z_ref, o_ref, kx_ref, vx_ref,
                      gssm_ref, gatt_ref, wos_ref, woa_ref, gpost_ref,
                      gxpre_ref, wxq_ref, wxo_ref, gxpost_ref, out_ref):
    z = z_ref[0]
    y = (yf_ref[0] + yb_ref[0]) * (z * _sigmoid(z))
    gw = SSM_INNER // SSM_GROUPS
    yn = jnp.concatenate(
        [y[:, g * gw:(g + 1) * gw]
         * lax.rsqrt(jnp.mean(y[:, g * gw:(g + 1) * gw] * y[:, g * gw:(g + 1) * gw], axis=-1, keepdims=True) + EPS)
         for g in range(SSM_GROUPS)], axis=1)
    y_ssm = (yn * gssm_ref[...]).astype(BF16)
    y_att = _rmsnorm(o_ref[0], gatt_ref[...]).astype(BF16)
    mix = _dot(y_ssm, wos_ref[...]) + _dot(y_att, woa_ref[...])
    x1 = x_ref[0] + _rmsnorm(mix, gpost_ref[...])

    h = _rmsnorm(x1, gxpre_ref[...]).astype(BF16)
    q = _dot(h, wxq_ref[...])
    scale = X_HEAD_DIM ** -0.5
    heads = []
    for hh in range(X_HEADS):
        cols = slice(hh * X_HEAD_DIM, (hh + 1) * X_HEAD_DIM)
        s = _dot_nt(q[:, cols].astype(BF16), kx_ref[0, :, cols]) * scale
        e = jnp.exp(s - jnp.max(s, axis=-1, keepdims=True))
        p = e / jnp.sum(e, axis=-1, keepdims=True)
        heads.append(_dot(p.astype(BF16), vx_ref[0, :, cols]))
    ca = _dot(jnp.concatenate(heads, axis=1).astype(BF16), wxo_ref[...])
    out_ref[0] = x1 + _rmsnorm(ca, gxpost_ref[...])


def _mix_xattn(x, yf, yb, z, o, kx, vx, gssm, gatt, wos, woa, gpost, gxpre, wxq, wxo, gxpost, tm):
    b, l, d = x.shape
    n_mem = kx.shape[1]
    tok = lambda n: pl.BlockSpec((1, tm, n), lambda bi, i: (bi, i, 0))
    mem = pl.BlockSpec((1, n_mem, X_INNER), lambda bi, i: (bi, 0, 0))
    consts = (gssm, gatt, wos, woa, gpost, gxpre, wxq, wxo, gxpost)
    return pl.pallas_call(
        _mix_xattn_kernel,
        grid=(b, l // tm),
        in_specs=[tok(d), tok(SSM_INNER), tok(SSM_INNER), tok(SSM_INNER), tok(ATTN_INNER), mem, mem]
                 + [_const_spec(c.shape) for c in consts],
        out_specs=tok(d),
        out_shape=jax.ShapeDtypeStruct((b, l, d), F32),
        compiler_params=_params("parallel", "parallel"),
        name="mix_xattn",
    )(x, yf, yb, z, o, kx, vx, *consts)


def _ffn_kernel(x_ref, gpre_ref, wg_ref, wu_ref, wd_ref, gpost_ref, out_ref, *, n_split):
    x = x_ref[0]
    h = _rmsnorm(x, gpre_ref[...]).astype(BF16)
    d_ff = wg_ref.shape[1]
    w = d_ff // n_split
    f = None
    for c in range(n_split):
        cols = slice(c * w, (c + 1) * w)
        gt = _dot(h, wg_ref[:, cols])
        up = _dot(h, wu_ref[:, cols])
        part = _dot((gt * _sigmoid(gt) * up).astype(BF16), wd_ref[cols, :])
        f = part if f is None else f + part
    out_ref[0] = x + _rmsnorm(f, gpost_ref[...])


def _ffn(x, gpre, wg, wu, wd, gpost, tm, n_split):
    b, l, d = x.shape
    tok = pl.BlockSpec((1, tm, d), lambda bi, i: (bi, i, 0))
    single = lambda a: pl.BlockSpec(a.shape, lambda *_: (0,) * a.ndim, pipeline_mode=pl.Buffered(1))
    return pl.pallas_call(
        functools.partial(_ffn_kernel, n_split=n_split),
        grid=(b, l // tm),
        in_specs=[tok, _const_spec(gpre.shape), single(wg), single(wu), single(wd), _const_spec(gpost.shape)],
        out_specs=tok,
        out_shape=jax.ShapeDtypeStruct((b, l, d), F32),
        compiler_params=_params("parallel", "parallel"),
        name="ffn",
    )(x, gpre, wg, wu, wd, gpost)


def _rope_tables(l):
    t = jnp.arange(l, dtype=jnp.int32)
    r = (t // GRID_W).astype(F32)
    c = (t % GRID_W).astype(F32)
    half = ATTN_HEAD_DIM // 2
    inv = 1.0 / (ROPE_THETA ** (jnp.arange(0, half, 2, dtype=F32) / half))
    ang_r = r[:, None] * inv
    ang_c = c[:, None] * inv
    cos = jnp.concatenate([jnp.cos(ang_r)] * 2 + [jnp.cos(ang_c)] * 2, axis=1)
    sin = jnp.concatenate([-jnp.sin(ang_r), jnp.sin(ang_r), -jnp.sin(ang_c), jnp.sin(ang_c)], axis=1)
    return jnp.tile(cos, (1, 2)), jnp.tile(sin, (1, 2))


def _tile(l, want):
    t = min(l, want)
    assert l % t == 0
    return t


def _layer(x, mem, w):
    b, l, _ = x.shape
    assert l % CHUNK == 0 and l % GRID_W == 0
    tm = _tile(l, 512)
    cos, sin = _rope_tables(l)
    z, xbc, dt, qp, kp, vt = _in_proj(x, w["g_pre"], w["wz"], w["wxbc"], w["wdt"], w["wq"], w["wk"], w["wv"],
                                      cos, sin, w["qg"], w["kg"], tm)
    xc = _conv(xbc, w["conv_w"], w["conv_b"], tm)
    yf, yb = _ssd(xc, dt, w["dt_bias"], w["a_log"], w["d_skip"])
    o = _attention(qp, kp, vt, tm, _tile(l, 512))
    kx, vx = _mem_kv(mem, w["g_mem"], w["wxk"], w["wxv"])
    x2 = _mix_xattn(x, yf, yb, z, o, kx, vx, w["g_ssm"], w["g_att"], w["wos"], w["woa"], w["g_post"],
                    w["g_xpre"], w["wxq"], w["wxo"], w["g_xpost"], tm)
    return _ffn(x2, w["g_fpre"], w["wg"], w["wu"], w["wd"], w["g_fpost"], tm, 2)


def _prep_weights(i, norm_mix_pre, w_in, conv_w, conv_b, dt_bias, a_log, d_skip, ssm_norm, q_norm, k_norm,
                  attn_norm, w_out, norm_mix_post, norm_x_pre, norm_mem, w_xq, w_xk, w_xv, w_xo, norm_x_post,
                  norm_ffn_pre, w_gate, w_up, w_down, norm_ffn_post):
    row = lambda v: v.reshape(1, -1).astype(F32)
    splits = np.cumsum([SSM_INNER, CONV_DIM, 2 * SSM_HEADS, ATTN_INNER,
                        ATTN_KV_HEADS * ATTN_HEAD_DIM, ATTN_KV_HEADS * ATTN_HEAD_DIM])[:-1]
    wz, wxbc, wdt, wq, wk, wv = jnp.split(w_in[i], [int(s) for s in splits], axis=-1)
    order = [h + ATTN_GROUP * g for h in range(ATTN_GROUP) for g in range(ATTN_KV_HEADS)]
    wq = wq.reshape(D_MODEL, ATTN_HEADS, ATTN_HEAD_DIM)[:, order, :].reshape(D_MODEL, ATTN_INNER)
    pad = DT_PAD - 2 * SSM_HEADS
    scale = ATTN_HEAD_DIM ** -0.5
    return {
        "g_pre": row(norm_mix_pre[i]),
        "wz": wz.astype(BF16), "wxbc": wxbc.astype(BF16),
        "wdt": jnp.pad(wdt, ((0, 0), (0, pad))).astype(BF16),
        "wq": wq.astype(BF16), "wk": wk.astype(BF16), "wv": wv.astype(BF16),
        "qg": jnp.tile(row(q_norm[i]) * scale, (1, LANES // ATTN_HEAD_DIM)),
        "kg": jnp.tile(row(k_norm[i]), (1, LANES // ATTN_HEAD_DIM)),
        "conv_w": conv_w[i].astype(F32), "conv_b": row(conv_b[i]),
        "dt_bias": jnp.pad(row(dt_bias[i]), ((0, 0), (0, pad))),
        "a_log": jnp.pad(row(a_log[i]), ((0, 0), (0, pad))),
        "d_skip": jnp.repeat(row(d_skip[i]), SSM_HEAD_DIM, axis=1),
        "g_ssm": row(ssm_norm[i]), "g_att": row(attn_norm[i]),
        "wos": w_out[i][:SSM_INNER].astype(BF16), "woa": w_out[i][SSM_INNER:].astype(BF16),
        "g_post": row(norm_mix_post[i]), "g_xpre": row(norm_x_pre[i]), "g_mem": row(norm_mem[i]),
        "wxq": w_xq[i].astype(BF16), "wxk": w_xk[i].astype(BF16), "wxv": w_xv[i].astype(BF16),
        "wxo": w_xo[i].astype(BF16), "g_xpost": row(norm_x_post[i]),
        "g_fpre": row(norm_ffn_pre[i]), "wg": w_gate[i].astype(BF16), "wu": w_up[i].astype(BF16),
        "wd": w_down[i].astype(BF16), "g_fpost": row(norm_ffn_post[i]),
    }


def kernel(x_prompt, x_sample, mem_prompt, mem_sample, norm_mix_pre, w_in, conv_w, conv_b, dt_bias, a_log,
           d_skip, ssm_norm, q_norm, k_norm, attn_norm, w_out, norm_mix_post, norm_x_pre, norm_mem, w_xq,
           w_xk, w_xv, w_xo, norm_x_post, norm_ffn_pre, w_gate, w_up, w_down, norm_ffn_post):
    y_prompt, y_sample = x_prompt, x_sample
    for i in range(w_in.shape[0]):
        w = _prep_weights(i, norm_mix_pre, w_in, conv_w, conv_b, dt_bias, a_log, d_skip, ssm_norm, q_norm,
                          k_norm, attn_norm, w_out, norm_mix_post, norm_x_pre, norm_mem, w_xq, w_xk, w_xv,
                          w_xo, norm_x_post, norm_ffn_pre, w_gate, w_up, w_down, norm_ffn_post)
        y_prompt = _layer(y_prompt, mem_prompt, w)
        y_sample = _layer(y_sample, mem_sample, w)
    return (y_prompt, y_sample)
```

```python
import functools

import numpy as np
import jax
import jax.numpy as jnp
from jax import lax
from jax.experimental import pallas as pl
from jax.experimental.pallas import tpu as pltpu

F32 = jnp.float32
BF16 = jnp.bfloat16

EPS = 1e-6
LANES = 128
SUBLANES = 8
VMEM_LIMIT_BYTES = 56 * 1024 * 1024

D_MODEL = 1024
SSM_HEADS = 16
SSM_HEAD_DIM = 64
SSM_INNER = SSM_HEADS * SSM_HEAD_DIM
SSM_GROUPS = 2
SSM_STATE = 128
D_CONV = 5
CHUNK = 128
CONV_DIM = SSM_INNER + 2 * SSM_GROUPS * SSM_STATE
ATTN_HEADS = 8
ATTN_KV_HEADS = 2
ATTN_HEAD_DIM = 64
ATTN_INNER = ATTN_HEADS * ATTN_HEAD_DIM
ATTN_GROUP = ATTN_HEADS // ATTN_KV_HEADS
GRID_W = 64
ROPE_THETA = 10000.0
X_HEADS = 4
X_HEAD_DIM = 128
X_INNER = X_HEADS * X_HEAD_DIM
DT_PAD = LANES
HALO = SUBLANES
MAX_EXP2_SCORE = 60.0
BOUND_MARGIN = 1.02


def _dot(a, b):
    return jnp.dot(a, b, preferred_element_type=F32)


def _dot_nt(a, b):
    return lax.dot_general(a, b, (((1,), (1,)), ((), ())), preferred_element_type=F32)


def _dot_tn(a, b):
    return lax.dot_general(a, b, (((0,), (0,)), ((), ())), preferred_element_type=F32)


def _rmsnorm(x, g):
    return x * lax.rsqrt(jnp.mean(x * x, axis=-1, keepdims=True) + EPS) * g


def _sigmoid(x):
    return 1.0 / (1.0 + jnp.exp(-x))


def _params(*sem):
    return pltpu.CompilerParams(dimension_semantics=sem, vmem_limit_bytes=VMEM_LIMIT_BYTES)


def _const_spec(shape):
    nd = len(shape)
    return pl.BlockSpec(shape, lambda *_: (0,) * nd)


def _in_proj_kernel(x_ref, g_ref, wz_ref, wxbc_ref, wdt_ref, wq_ref, wk_ref, wv_ref,
                    cos_ref, sin_ref, qg_ref, kg_ref,
                    z_ref, xbc_ref, dt_ref, qp_ref, kp_ref, vt_ref):
    x = x_ref[0]
    h = _rmsnorm(x, g_ref[...]).astype(BF16)
    z_ref[0] = _dot(h, wz_ref[...])
    xbc_ref[0] = _dot(h, wxbc_ref[...])
    dt_ref[0] = _dot(h, wdt_ref[...])
    q = _dot(h, wq_ref[...])
    k = _dot(h, wk_ref[...])
    v = _dot(h, wv_ref[...])
    vt_ref[0] = v.T.astype(BF16)

    cos = cos_ref[...]
    sin = sin_ref[...]
    lane = lax.broadcasted_iota(jnp.int32, cos.shape, 1)

    r_id = lax.broadcasted_iota(jnp.int32, (LANES, LANES), 0) // ATTN_HEAD_DIM
    c_id = lax.broadcasted_iota(jnp.int32, (LANES, LANES), 1) // ATTN_HEAD_DIM
    same_head = jnp.where(r_id == c_id, 1.0, 0.0).astype(BF16)

    def head_norm_rope(blk, gain):
        sq = blk * blk
        hi = sq.astype(BF16)
        lo = (sq - hi.astype(F32)).astype(BF16)
        ss = _dot(hi, same_head) + _dot(lo, same_head)
        y = blk * lax.rsqrt(ss * (1.0 / ATTN_HEAD_DIM) + EPS) * gain
        partner = jnp.where((lane & 16) != 0, pltpu.roll(y, 16, 1), pltpu.roll(y, LANES - 16, 1))
        return y * cos + partner * sin

    kp_ref[0] = head_norm_rope(k, kg_ref[...]).astype(BF16)
    low = lane < ATTN_HEAD_DIM
    qg = qg_ref[...]
    for j in range(ATTN_GROUP):
        r = head_norm_rope(q[:, j * LANES:(j + 1) * LANES], qg)
        qp_ref[0, :, j * LANES:(j + 1) * LANES] = jnp.where(low, r, 0.0).astype(BF16)
        qp_ref[0, :, (ATTN_GROUP + j) * LANES:(ATTN_GROUP + j + 1) * LANES] = jnp.where(low, 0.0, r).astype(BF16)


def _in_proj(x, g, wz, wxbc, wdt, wq, wk, wv, cos, sin, qg, kg, tm):
    b, l, d = x.shape
    grid = (b, l // tm)
    tok = lambda n: pl.BlockSpec((1, tm, n), lambda bi, i: (bi, i, 0))
    pos = pl.BlockSpec((tm, LANES), lambda bi, i: (i, 0))
    out_shape = (
        jax.ShapeDtypeStruct((b, l, SSM_INNER), F32),
        jax.ShapeDtypeStruct((b, l, CONV_DIM), F32),
        jax.ShapeDtypeStruct((b, l, DT_PAD), F32),
        jax.ShapeDtypeStruct((b, l, ATTN_HEADS * LANES), BF16),
        jax.ShapeDtypeStruct((b, l, LANES), BF16),
        jax.ShapeDtypeStruct((b, LANES, l), BF16),
    )
    return pl.pallas_call(
        _in_proj_kernel,
        grid=grid,
        in_specs=[tok(d), _const_spec(g.shape), _const_spec(wz.shape), _const_spec(wxbc.shape),
                  _const_spec(wdt.shape), _const_spec(wq.shape), _const_spec(wk.shape),
                  _const_spec(wv.shape), pos, pos, _const_spec(qg.shape), _const_spec(kg.shape)],
        out_specs=(tok(SSM_INNER), tok(CONV_DIM), tok(DT_PAD), tok(ATTN_HEADS * LANES), tok(LANES),
                   pl.BlockSpec((1, LANES, tm), lambda bi, i: (bi, 0, i))),
        out_shape=out_shape,
        compiler_params=_params("parallel", "parallel"),
        name="in_proj",
    )(x, g, wz, wxbc, wdt, wq, wk, wv, cos, sin, qg, kg)


def _conv_kernel(xm_ref, xp_ref, xn_ref, w_ref, b_ref, o_ref, ext_ref, *, tc):
    i = pl.program_id(1)
    n = pl.num_programs(1)
    ext_ref[0:HALO, :] = jnp.where(i > 0, xp_ref[0], 0.0)
    ext_ref[HALO:HALO + tc, :] = xm_ref[0]
    ext_ref[HALO + tc:HALO + tc + HALO, :] = jnp.where(i < n - 1, xn_ref[0], 0.0)
    acc = jnp.broadcast_to(b_ref[...], (tc, CONV_DIM))
    for j in range(D_CONV):
        off = HALO - D_CONV // 2 + j
        acc = acc + ext_ref[off:off + tc, :] * w_ref[j:j + 1, :]
    o_ref[0] = (acc * _sigmoid(acc)).astype(BF16)


def _conv(xbc, w, bias, tc):
    b, l, c = xbc.shape
    nh = tc // HALO
    last = l // HALO - 1
    return pl.pallas_call(
        functools.partial(_conv_kernel, tc=tc),
        grid=(b, l // tc),
        in_specs=[pl.BlockSpec((1, tc, c), lambda bi, i: (bi, i, 0)),
                  pl.BlockSpec((1, HALO, c), lambda bi, i: (bi, jnp.maximum(i * nh - 1, 0), 0)),
                  pl.BlockSpec((1, HALO, c), lambda bi, i: (bi, jnp.minimum((i + 1) * nh, last), 0)),
                  _const_spec(w.shape), _const_spec(bias.shape)],
        out_specs=pl.BlockSpec((1, tc, c), lambda bi, i: (bi, i, 0)),
        out_shape=jax.ShapeDtypeStruct((b, l, c), BF16),
        scratch_shapes=[pltpu.VMEM((tc + 2 * HALO, c), F32)],
        compiler_params=_params("parallel", "parallel"),
        name="conv",
    )(xbc, xbc, xbc, w, bias)


def _expand_heads(w, sel):
    hi = w.astype(BF16)
    lo = (w - hi.astype(F32)).astype(BF16)
    return _dot(hi, sel) + _dot(lo, sel)


def _ssd_direction(xc, dt_raw, dtb, a_neg, sel, h_ref, reverse):
    off = SSM_HEADS if reverse else 0
    row = lax.broadcasted_iota(jnp.int32, (CHUNK, CHUNK), 0)
    col = lax.broadcasted_iota(jnp.int32, (CHUNK, CHUNK), 1)
    mask = (row <= col) if reverse else (row >= col)
    tri = mask.astype(F32)
    xpre = dt_raw + dtb
    dtv = jnp.maximum(xpre, 0.0) + jnp.log1p(jnp.exp(-jnp.abs(xpre)))
    a = dtv * a_neg
    cs = jnp.dot(tri, a, preferred_element_type=F32, precision=lax.Precision.HIGHEST)
    end = 0 if reverse else CHUNK - 1
    cs_end = cs[end:end + 1, :]
    e_cs = jnp.exp(cs)
    w_state = dtv * jnp.exp(cs_end - cs)
    cs_t = cs.T
    dt_t = dtv.T
    e3 = _expand_heads(e_cs, sel)
    e2 = _expand_heads(w_state, sel)
    xs = xc[:, :SSM_INNER]
    xdec = (xs.astype(F32) * e2).astype(BF16)
    lane = lax.broadcasted_iota(jnp.int32, (CHUNK, LANES), 1)
    low = lane < SSM_HEAD_DIM
    zero = jnp.zeros((CHUNK, LANES), BF16)
    gw = SSM_INNER // SSM_GROUPS
    hpg = SSM_HEADS // SSM_GROUPS
    y_tiles = []
    for g in range(SSM_GROUPS):
        bm = xc[:, SSM_INNER + g * SSM_STATE:SSM_INNER + (g + 1) * SSM_STATE]
        cm = xc[:, SSM_INNER + (SSM_GROUPS + g) * SSM_STATE:SSM_INNER + (SSM_GROUPS + g + 1) * SSM_STATE]
        cb = _dot_nt(cm, bm)
        h_prev = h_ref[:, g * gw:(g + 1) * gw]
        y_off = _dot(cm, h_prev.astype(BF16)) * e3[:, g * gw:(g + 1) * gw]
        for jj in range(hpg // 2):
            j = g * (hpg // 2) + jj
            x_tile = xs[:, j * LANES:(j + 1) * LANES]
            acc = y_off[:, jj * LANES:(jj + 1) * LANES]
            for half in range(2):
                hd = off + 2 * j + half
                seg = cs[:, hd:hd + 1] - cs_t[hd:hd + 1, :]
                m = cb * jnp.exp(jnp.where(mask, seg, -jnp.inf)) * dt_t[hd:hd + 1, :]
                x_half = jnp.where(low, x_tile, zero) if half == 0 else jnp.where(low, zero, x_tile)
                acc = acc + _dot(m.astype(BF16), x_half)
            y_tiles.append(acc)
        s_new = _dot_tn(bm, xdec[:, g * gw:(g + 1) * gw])
        h_ref[:, g * gw:(g + 1) * gw] = h_prev * e3[end:end + 1, g * gw:(g + 1) * gw] + s_new
    return jnp.concatenate(y_tiles, axis=1)


def _ssd_kernel(xf_ref, xb_ref, dtf_ref, dtb_ref, bias_ref, alog_ref, dskip_ref, sel_ref,
                yf_ref, yb_ref, hf_ref, hb_ref):
    c = pl.program_id(1)

    @pl.when(c == 0)
    def _():
        hf_ref[...] = jnp.zeros_like(hf_ref)
        hb_ref[...] = jnp.zeros_like(hb_ref)

    a_neg = -jnp.exp(alog_ref[...])
    bias = bias_ref[...]
    xf = xf_ref[0]
    yf = _ssd_direction(xf, dtf_ref[0], bias, a_neg, sel_ref[0], hf_ref, reverse=False)
    yf_ref[0] = yf + xf[:, :SSM_INNER].astype(F32) * dskip_ref[...]
    yb_ref[0] = _ssd_direction(xb_ref[0], dtb_ref[0], bias, a_neg, sel_ref[1], hb_ref, reverse=True)


def _ssd(xc, dt, bias, alog, dskip):
    b, l, _ = xc.shape
    nc = l // CHUNK
    fwd = lambda n: pl.BlockSpec((1, CHUNK, n), lambda bi, c: (bi, c, 0))
    bwd = lambda n: pl.BlockSpec((1, CHUNK, n), lambda bi, c: (bi, nc - 1 - c, 0))
    lane_head = np.arange(SSM_INNER) // SSM_HEAD_DIM
    sel = np.stack([np.arange(DT_PAD)[:, None] == (d * SSM_HEADS + lane_head)[None, :] for d in range(2)])
    sel = jnp.asarray(sel, BF16)
    return pl.pallas_call(
        _ssd_kernel,
        grid=(b, nc),
        in_specs=[fwd(CONV_DIM), bwd(CONV_DIM), fwd(DT_PAD), bwd(DT_PAD),
                  _const_spec(bias.shape), _const_spec(alog.shape), _const_spec(dskip.shape),
                  _const_spec(sel.shape)],
        out_specs=(fwd(SSM_INNER), bwd(SSM_INNER)),
        out_shape=(jax.ShapeDtypeStruct((b, l, SSM_INNER), F32),
                   jax.ShapeDtypeStruct((b, l, SSM_INNER), F32)),
        scratch_shapes=[pltpu.VMEM((SSM_STATE, SSM_INNER), F32),
                        pltpu.VMEM((SSM_STATE, SSM_INNER), F32)],
        compiler_params=_params("parallel", "arbitrary"),
        name="ssd",
    )(xc, xc, dt, dt, bias, alog, dskip, sel)


def _attn_kernel(bounded_ref, q_ref, k_ref, vt_ref, o_ref, m_ref, l_ref, acc_ref, *, tk):
    nk = k_ref.shape[1] // tk
    l_ref[...] = jnp.zeros_like(l_ref)
    acc_ref[...] = jnp.zeros_like(acc_ref)

    def tiles(i):
        ks = pl.multiple_of(i * tk, tk)
        return k_ref[0, pl.ds(ks, tk), :], vt_ref[0, :, pl.ds(ks, tk)]

    def bounded_body(i, carry):
        k, vt = tiles(i)
        for j in range(ATTN_GROUP):
            rows = slice(j * ATTN_HEAD_DIM, (j + 1) * ATTN_HEAD_DIM)
            p_t = jnp.exp2(_dot_nt(k, q_ref[0, :, j * LANES:(j + 1) * LANES]))
            part = p_t[0:SUBLANES, :]
            for r in range(1, tk // SUBLANES):
                part = part + p_t[r * SUBLANES:(r + 1) * SUBLANES, :]
            l_ref[j * SUBLANES:(j + 1) * SUBLANES, :] += part
            acc_ref[rows, :] += _dot(vt, p_t.astype(BF16))
        return carry

    def online_body(i, carry):
        k, vt = tiles(i)
        for j in range(ATTN_GROUP):
            rows = slice(j * ATTN_HEAD_DIM, (j + 1) * ATTN_HEAD_DIM)
            s_t = _dot_nt(k, q_ref[0, :, j * LANES:(j + 1) * LANES])
            m_prev = m_ref[j:j + 1, :]
            m_new = jnp.maximum(m_prev, jnp.max(s_t, axis=0, keepdims=True))
            alpha = jnp.exp2(m_prev - m_new)
            p_t = jnp.exp2(s_t - m_new)
            lrow = slice(j * SUBLANES, j * SUBLANES + 1)
            l_ref[lrow, :] = alpha * l_ref[lrow, :] + jnp.sum(p_t, axis=0, keepdims=True)
            acc_ref[rows, :] = alpha * acc_ref[rows, :] + _dot(vt, p_t.astype(BF16))
            m_ref[j:j + 1, :] = m_new
        return carry

    bounded = bounded_ref[0] != 0

    @pl.when(bounded)
    def _():
        lax.fori_loop(0, nk, bounded_body, 0)

    @pl.when(jnp.logical_not(bounded))
    def _():
        m_ref[...] = jnp.full_like(m_ref, -jnp.inf)
        lax.fori_loop(0, nk, online_body, 0)

    for j in range(ATTN_GROUP):
        rows = slice(j * ATTN_HEAD_DIM, (j + 1) * ATTN_HEAD_DIM)
        l = jnp.sum(l_ref[j * SUBLANES:(j + 1) * SUBLANES, :], axis=0, keepdims=True)
        acc_ref[rows, :] = acc_ref[rows, :] / l
    o_ref[0] = acc_ref[...].T


def _attention(bounded, qp, kp, vt, tq, tk):
    b, l, _ = qp.shape
    gw = ATTN_GROUP * LANES
    return pl.pallas_call(
        functools.partial(_attn_kernel, tk=tk),
        grid=(b, ATTN_KV_HEADS, l // tq),
        in_specs=[pl.BlockSpec(memory_space=pltpu.SMEM),
                  pl.BlockSpec((1, tq, gw), lambda bi, g, i: (bi, i, g)),
                  pl.BlockSpec((1, l, LANES), lambda bi, g, i: (bi, 0, 0)),
                  pl.BlockSpec((1, ATTN_HEAD_DIM, l), lambda bi, g, i: (bi, g, 0))],
        out_specs=pl.BlockSpec((1, tq, ATTN_GROUP * ATTN_HEAD_DIM), lambda bi, g, i: (bi, i, g)),
        out_shape=jax.ShapeDtypeStruct((b, l, ATTN_INNER), F32),
        scratch_shapes=[pltpu.VMEM((SUBLANES, tq), F32), pltpu.VMEM((ATTN_GROUP * SUBLANES, tq), F32),
                        pltpu.VMEM((ATTN_GROUP * ATTN_HEAD_DIM, tq), F32)],
        compiler_params=_params("parallel", "parallel", "parallel"),
        name="attention",
    )(bounded, qp, kp, vt)


def _mem_kv_kernel(mem_ref, g_ref, wk_ref, wv_ref, k_ref, v_ref):
    m = _rmsnorm(mem_ref[0], g_ref[...]).astype(BF16)
    k_ref[0] = _dot(m, wk_ref[...]).astype(BF16)
    v_ref[0] = _dot(m, wv_ref[...]).astype(BF16)


def _mem_kv(mem, g, wk, wv):
    b, n, d = mem.shape
    spec = pl.BlockSpec((1, n, X_INNER), lambda bi: (bi, 0, 0))
    return pl.pallas_call(
        _mem_kv_kernel,
        grid=(b,),
        in_specs=[pl.BlockSpec((1, n, d), lambda bi: (bi, 0, 0)), _const_spec(g.shape),
                  _const_spec(wk.shape), _const_spec(wv.shape)],
        out_specs=(spec, spec),
        out_shape=(jax.ShapeDtypeStruct((b, n, X_INNER), BF16),) * 2,
        compiler_params=_params("parallel"),
        name="mem_kv",
    )(mem, g, wk, wv)


def _mix_xattn_kernel(x_ref, yf_ref, yb_ref, z_ref, o_ref, kx_ref, vx_ref,
                      gssm_ref, gatt_ref, wos_ref, woa_ref, gpost_ref,
                      gxpre_ref, wxq_ref, wxo_ref, gxpost_ref, out_ref):
    z = z_ref[0]
    y = (yf_ref[0] + yb_ref[0]) * (z * _sigmoid(z))
    gw = SSM_INNER // SSM_GROUPS
    yn = jnp.concatenate(
        [y[:, g * gw:(g + 1) * gw]
         * lax.rsqrt(jnp.mean(y[:, g * gw:(g + 1) * gw] * y[:, g * gw:(g + 1) * gw], axis=-1, keepdims=True) + EPS)
         for g in range(SSM_GROUPS)], axis=1)
    y_ssm = (yn * gssm_ref[...]).astype(BF16)
    y_att = _rmsnorm(o_ref[0], gatt_ref[...]).astype(BF16)
    mix = _dot(y_ssm, wos_ref[...]) + _dot(y_att, woa_ref[...])
    x1 = x_ref[0] + _rmsnorm(mix, gpost_ref[...])

    h = _rmsnorm(x1, gxpre_ref[...]).astype(BF16)
    q = _dot(h, wxq_ref[...])
    scale = X_HEAD_DIM ** -0.5
    heads = []
    for hh in range(X_HEADS):
        cols = slice(hh * X_HEAD_DIM, (hh + 1) * X_HEAD_DIM)
        s = _dot_nt(q[:, cols].astype(BF16), kx_ref[0, :, cols]) * scale
        e = jnp.exp(s - jnp.max(s, axis=-1, keepdims=True))
        p = e / jnp.sum(e, axis=-1, keepdims=True)
        heads.append(_dot(p.astype(BF16), vx_ref[0, :, cols]))
    ca = _dot(jnp.concatenate(heads, axis=1).astype(BF16), wxo_ref[...])
    out_ref[0] = x1 + _rmsnorm(ca, gxpost_ref[...])


def _mix_xattn(x, yf, yb, z, o, kx, vx, gssm, gatt, wos, woa, gpost, gxpre, wxq, wxo, gxpost, tm):
    b, l, d = x.shape
    n_mem = kx.shape[1]
    tok = lambda n: pl.BlockSpec((1, tm, n), lambda bi, i: (bi, i, 0))
    mem = pl.BlockSpec((1, n_mem, X_INNER), lambda bi, i: (bi, 0, 0))
    consts = (gssm, gatt, wos, woa, gpost, gxpre, wxq, wxo, gxpost)
    return pl.pallas_call(
        _mix_xattn_kernel,
        grid=(b, l // tm),
        in_specs=[tok(d), tok(SSM_INNER), tok(SSM_INNER), tok(SSM_INNER), tok(ATTN_INNER), mem, mem]
                 + [_const_spec(c.shape) for c in consts],
        out_specs=tok(d),
        out_shape=jax.ShapeDtypeStruct((b, l, d), F32),
        compiler_params=_params("parallel", "parallel"),
        name="mix_xattn",
    )(x, yf, yb, z, o, kx, vx, *consts)


def _ffn_kernel(x_ref, gpre_ref, wg_ref, wu_ref, wd_ref, gpost_ref, out_ref, *, n_split):
    x = x_ref[0]
    h = _rmsnorm(x, gpre_ref[...]).astype(BF16)
    d_ff = wg_ref.shape[1]
    w = d_ff // n_split
    f = None
    for c in range(n_split):
        cols = slice(c * w, (c + 1) * w)
        gt = _dot(h, wg_ref[:, cols])
        up = _dot(h, wu_ref[:, cols])
        part = _dot((gt * _sigmoid(gt) * up).astype(BF16), wd_ref[cols, :])
        f = part if f is None else f + part
    out_ref[0] = x + _rmsnorm(f, gpost_ref[...])


def _ffn(x, gpre, wg, wu, wd, gpost, tm, n_split):
    b, l, d = x.shape
    tok = pl.BlockSpec((1, tm, d), lambda bi, i: (bi, i, 0))
    single = lambda a: pl.BlockSpec(a.shape, lambda *_: (0,) * a.ndim, pipeline_mode=pl.Buffered(1))
    return pl.pallas_call(
        functools.partial(_ffn_kernel, n_split=n_split),
        grid=(b, l // tm),
        in_specs=[tok, _const_spec(gpre.shape), single(wg), single(wu), single(wd), _const_spec(gpost.shape)],
        out_specs=tok,
        out_shape=jax.ShapeDtypeStruct((b, l, d), F32),
        compiler_params=_params("parallel", "parallel"),
        name="ffn",
    )(x, gpre, wg, wu, wd, gpost)


def _rope_tables(l):
    t = jnp.arange(l, dtype=jnp.int32)
    r = (t // GRID_W).astype(F32)
    c = (t % GRID_W).astype(F32)
    half = ATTN_HEAD_DIM // 2
    inv = 1.0 / (ROPE_THETA ** (jnp.arange(0, half, 2, dtype=F32) / half))
    ang_r = r[:, None] * inv
    ang_c = c[:, None] * inv
    cos = jnp.concatenate([jnp.cos(ang_r)] * 2 + [jnp.cos(ang_c)] * 2, axis=1)
    sin = jnp.concatenate([-jnp.sin(ang_r), jnp.sin(ang_r), -jnp.sin(ang_c), jnp.sin(ang_c)], axis=1)
    return jnp.tile(cos, (1, 2)), jnp.tile(sin, (1, 2))


def _tile(l, want):
    t = min(l, want)
    assert l % t == 0
    return t


def _layer(x, mem, w):
    b, l, _ = x.shape
    assert l % CHUNK == 0 and l % GRID_W == 0
    tm = _tile(l, 512)
    cos, sin = _rope_tables(l)
    z, xbc, dt, qp, kp, vt = _in_proj(x, w["g_pre"], w["wz"], w["wxbc"], w["wdt"], w["wq"], w["wk"], w["wv"],
                                      cos, sin, w["qg"], w["kg"], tm)
    xc = _conv(xbc, w["conv_w"], w["conv_b"], tm)
    yf, yb = _ssd(xc, dt, w["dt_bias"], w["a_log"], w["d_skip"])
    o = _attention(w["attn_bounded"], qp, kp, vt, _tile(l, 1024), _tile(l, 1024))
    kx, vx = _mem_kv(mem, w["g_mem"], w["wxk"], w["wxv"])
    x2 = _mix_xattn(x, yf, yb, z, o, kx, vx, w["g_ssm"], w["g_att"], w["wos"], w["woa"], w["g_post"],
                    w["g_xpre"], w["wxq"], w["wxo"], w["g_xpost"], tm)
    return _ffn(x2, w["g_fpre"], w["wg"], w["wu"], w["wd"], w["g_fpost"], tm, 2)


def _prep_weights(i, norm_mix_pre, w_in, conv_w, conv_b, dt_bias, a_log, d_skip, ssm_norm, q_norm, k_norm,
                  attn_norm, w_out, norm_mix_post, norm_x_pre, norm_mem, w_xq, w_xk, w_xv, w_xo, norm_x_post,
                  norm_ffn_pre, w_gate, w_up, w_down, norm_ffn_post):
    row = lambda v: v.reshape(1, -1).astype(F32)
    splits = np.cumsum([SSM_INNER, CONV_DIM, 2 * SSM_HEADS, ATTN_INNER,
                        ATTN_KV_HEADS * ATTN_HEAD_DIM, ATTN_KV_HEADS * ATTN_HEAD_DIM])[:-1]
    wz, wxbc, wdt, wq, wk, wv = jnp.split(w_in[i], [int(s) for s in splits], axis=-1)
    order = [h + ATTN_GROUP * g for h in range(ATTN_GROUP) for g in range(ATTN_KV_HEADS)]
    wq = wq.reshape(D_MODEL, ATTN_HEADS, ATTN_HEAD_DIM)[:, order, :].reshape(D_MODEL, ATTN_INNER)
    pad = DT_PAD - 2 * SSM_HEADS
    q_gain = row(q_norm[i]) * (ATTN_HEAD_DIM ** -0.5 * np.log2(np.e))
    k_gain = row(k_norm[i])
    score_bound = ATTN_HEAD_DIM * jnp.max(jnp.abs(q_gain)) * jnp.max(jnp.abs(k_gain))
    return {
        "attn_bounded": (score_bound * BOUND_MARGIN <= MAX_EXP2_SCORE).astype(jnp.int32).reshape(1),
        "g_pre": row(norm_mix_pre[i]),
        "wz": wz.astype(BF16), "wxbc": wxbc.astype(BF16),
        "wdt": jnp.pad(wdt, ((0, 0), (0, pad))).astype(BF16),
        "wq": wq.astype(BF16), "wk": wk.astype(BF16), "wv": wv.astype(BF16),
        "qg": jnp.tile(q_gain, (1, LANES // ATTN_HEAD_DIM)),
        "kg": jnp.tile(k_gain, (1, LANES // ATTN_HEAD_DIM)),
        "conv_w": conv_w[i].astype(F32), "conv_b": row(conv_b[i]),
        "dt_bias": jnp.pad(row(dt_bias[i]), ((0, 0), (0, pad))),
        "a_log": jnp.pad(row(a_log[i]), ((0, 0), (0, pad))),
        "d_skip": jnp.repeat(row(d_skip[i]), SSM_HEAD_DIM, axis=1),
        "g_ssm": row(ssm_norm[i]), "g_att": row(attn_norm[i]),
        "wos": w_out[i][:SSM_INNER].astype(BF16), "woa": w_out[i][SSM_INNER:].astype(BF16),
        "g_post": row(norm_mix_post[i]), "g_xpre": row(norm_x_pre[i]), "g_mem": row(norm_mem[i]),
        "wxq": w_xq[i].astype(BF16), "wxk": w_xk[i].astype(BF16), "wxv": w_xv[i].astype(BF16),
        "wxo": w_xo[i].astype(BF16), "g_xpost": row(norm_x_post[i]),
        "g_fpre": row(norm_ffn_pre[i]), "wg": w_gate[i].astype(BF16), "wu": w_up[i].astype(BF16),
        "wd": w_down[i].astype(BF16), "g_fpost": row(norm_ffn_post[i]),
    }


def kernel(x_prompt, x_sample, mem_prompt, mem_sample, norm_mix_pre, w_in, conv_w, conv_b, dt_bias, a_log,
           d_skip, ssm_norm, q_norm, k_norm, attn_norm, w_out, norm_mix_post, norm_x_pre, norm_mem, w_xq,
           w_xk, w_xv, w_xo, norm_x_post, norm_ffn_pre, w_gate, w_up, w_down, norm_ffn_post):
    y_prompt, y_sample = x_prompt, x_sample
    for i in range(w_in.shape[0]):
        w = _prep_weights(i, norm_mix_pre, w_in, conv_w, conv_b, dt_bias, a_log, d_skip, ssm_norm, q_norm,
                          k_norm, attn_norm, w_out, norm_mix_post, norm_x_pre, norm_mem, w_xq, w_xk, w_xv,
                          w_xo, norm_x_post, norm_ffn_pre, w_gate, w_up, w_down, norm_ffn_post)
        y_prompt = _layer(y_prompt, mem_prompt, w)
        y_sample = _layer(y_sample, mem_sample, w)
    return (y_prompt, y_sample)
```

```python
import functools

import numpy as np
import jax
import jax.numpy as jnp
from jax import lax
from jax.experimental import pallas as pl
from jax.experimental.pallas import tpu as pltpu

F32 = jnp.float32
BF16 = jnp.bfloat16

EPS = 1e-6
LANES = 128
SUBLANES = 8
VMEM_LIMIT_BYTES = 56 * 1024 * 1024

D_MODEL = 1024
SSM_HEADS = 16
SSM_HEAD_DIM = 64
SSM_INNER = SSM_HEADS * SSM_HEAD_DIM
SSM_GROUPS = 2
SSM_STATE = 128
D_CONV = 5
CHUNK = 128
CONV_DIM = SSM_INNER + 2 * SSM_GROUPS * SSM_STATE
ATTN_HEADS = 8
ATTN_KV_HEADS = 2
ATTN_HEAD_DIM = 64
ATTN_INNER = ATTN_HEADS * ATTN_HEAD_DIM
ATTN_GROUP = ATTN_HEADS // ATTN_KV_HEADS
GRID_W = 64
ROPE_THETA = 10000.0
X_HEADS = 4
X_HEAD_DIM = 128
X_INNER = X_HEADS * X_HEAD_DIM
DT_PAD = LANES
HALO = SUBLANES
MAX_EXP2_SCORE = 60.0
BOUND_MARGIN = 1.02


def _dot(a, b):
    return jnp.dot(a, b, preferred_element_type=F32)


def _dot_nt(a, b):
    return lax.dot_general(a, b, (((1,), (1,)), ((), ())), preferred_element_type=F32)


def _dot_tn(a, b):
    return lax.dot_general(a, b, (((0,), (0,)), ((), ())), preferred_element_type=F32)


def _rmsnorm(x, g):
    return x * lax.rsqrt(jnp.mean(x * x, axis=-1, keepdims=True) + EPS) * g


def _sigmoid(x):
    return 1.0 / (1.0 + jnp.exp(-x))


def _params(*sem):
    return pltpu.CompilerParams(dimension_semantics=sem, vmem_limit_bytes=VMEM_LIMIT_BYTES)


def _const_spec(shape):
    nd = len(shape)
    return pl.BlockSpec(shape, lambda *_: (0,) * nd)


def _in_proj_kernel(x_ref, xp_ref, xn_ref, g_ref, wz_ref, wxbc_ref, wdt_ref, wq_ref, wk_ref, wv_ref,
                    cos_ref, sin_ref, qg_ref, kg_ref, cw_ref, cb_ref,
                    z_ref, xc_ref, dt_ref, qp_ref, kp_ref, vt_ref, ext_ref):
    i = pl.program_id(1)
    n = pl.num_programs(1)
    tm = x_ref.shape[1]
    x_all = jnp.concatenate([x_ref[0], xp_ref[0], xn_ref[0]], axis=0)
    h_all = _rmsnorm(x_all, g_ref[...]).astype(BF16)
    h = h_all[:tm]
    z_ref[0] = _dot(h, wz_ref[...]).astype(BF16)
    dt_ref[0] = _dot(h, wdt_ref[...])

    xbc = _dot(h_all, wxbc_ref[...])
    ext_ref[0:HALO, :] = jnp.where(i > 0, xbc[tm:tm + HALO], 0.0)
    ext_ref[HALO:HALO + tm, :] = xbc[:tm]
    ext_ref[HALO + tm:HALO + tm + HALO, :] = jnp.where(i < n - 1, xbc[tm + HALO:], 0.0)
    acc = jnp.broadcast_to(cb_ref[...], (tm, CONV_DIM))
    for j in range(D_CONV):
        off = HALO - D_CONV // 2 + j
        acc = acc + ext_ref[off:off + tm, :] * cw_ref[j:j + 1, :]
    xc_ref[0] = (acc * _sigmoid(acc)).astype(BF16)

    q = _dot(h, wq_ref[...])
    k = _dot(h, wk_ref[...])
    v = _dot(h, wv_ref[...])
    vt_ref[0] = v.T.astype(BF16)

    cos = cos_ref[...]
    sin = sin_ref[...]
    lane = lax.broadcasted_iota(jnp.int32, cos.shape, 1)

    r_id = (lax.broadcasted_iota(jnp.int32, (2 * LANES, LANES), 0) % LANES) // ATTN_HEAD_DIM
    c_id = lax.broadcasted_iota(jnp.int32, (2 * LANES, LANES), 1) // ATTN_HEAD_DIM
    same_head = jnp.where(r_id == c_id, 1.0, 0.0).astype(BF16)

    def head_norm_rope(blk, gain):
        sq = blk * blk
        hi = sq.astype(BF16)
        lo = (sq - hi.astype(F32)).astype(BF16)
        ss = _dot(jnp.concatenate([hi, lo], axis=1), same_head)
        y = blk * lax.rsqrt(ss * (1.0 / ATTN_HEAD_DIM) + EPS) * gain
        partner = jnp.where((lane & 16) != 0, pltpu.roll(y, 16, 1), pltpu.roll(y, LANES - 16, 1))
        return y * cos + partner * sin

    kp_ref[0] = head_norm_rope(k, kg_ref[...]).astype(BF16)
    low = lane < ATTN_HEAD_DIM
    qg = qg_ref[...]
    for j in range(ATTN_GROUP):
        r = head_norm_rope(q[:, j * LANES:(j + 1) * LANES], qg)
        qp_ref[0, :, j * LANES:(j + 1) * LANES] = jnp.where(low, r, 0.0).astype(BF16)
        qp_ref[0, :, (ATTN_GROUP + j) * LANES:(ATTN_GROUP + j + 1) * LANES] = jnp.where(low, 0.0, r).astype(BF16)


def _in_proj(x, g, wz, wxbc, wdt, wq, wk, wv, cos, sin, qg, kg, cw, cb, tm):
    b, l, d = x.shape
    grid = (b, l // tm)
    nh = tm // HALO
    last = l // HALO - 1
    tok = lambda n: pl.BlockSpec((1, tm, n), lambda bi, i: (bi, i, 0))
    pos = pl.BlockSpec((tm, LANES), lambda bi, i: (i, 0))
    prev_rows = pl.BlockSpec((1, HALO, d), lambda bi, i: (bi, jnp.maximum(i * nh - 1, 0), 0))
    next_rows = pl.BlockSpec((1, HALO, d), lambda bi, i: (bi, jnp.minimum((i + 1) * nh, last), 0))
    out_shape = (
        jax.ShapeDtypeStruct((b, l, SSM_INNER), BF16),
        jax.ShapeDtypeStruct((b, l, CONV_DIM), BF16),
        jax.ShapeDtypeStruct((b, l, DT_PAD), F32),
        jax.ShapeDtypeStruct((b, l, ATTN_HEADS * LANES), BF16),
        jax.ShapeDtypeStruct((b, l, LANES), BF16),
        jax.ShapeDtypeStruct((b, LANES, l), BF16),
    )
    return pl.pallas_call(
        _in_proj_kernel,
        grid=grid,
        in_specs=[tok(d), prev_rows, next_rows, _const_spec(g.shape), _const_spec(wz.shape),
                  _const_spec(wxbc.shape), _const_spec(wdt.shape), _const_spec(wq.shape), _const_spec(wk.shape),
                  _const_spec(wv.shape), pos, pos, _const_spec(qg.shape), _const_spec(kg.shape),
                  _const_spec(cw.shape), _const_spec(cb.shape)],
        out_specs=(tok(SSM_INNER), tok(CONV_DIM), tok(DT_PAD), tok(ATTN_HEADS * LANES), tok(LANES),
                   pl.BlockSpec((1, LANES, tm), lambda bi, i: (bi, 0, i))),
        out_shape=out_shape,
        scratch_shapes=[pltpu.VMEM((tm + 2 * HALO, CONV_DIM), F32)],
        compiler_params=_params("parallel", "parallel"),
        name="in_proj",
    )(x, x, x, g, wz, wxbc, wdt, wq, wk, wv, cos, sin, qg, kg, cw, cb)


def _expand_heads(w, sel):
    hi = w.astype(BF16)
    lo = (w - hi.astype(F32)).astype(BF16)
    return _dot(jnp.concatenate([hi, lo], axis=1), sel)


def _chunk_mask(reverse):
    row = lax.broadcasted_iota(jnp.int32, (CHUNK, CHUNK), 0)
    col = lax.broadcasted_iota(jnp.int32, (CHUNK, CHUNK), 1)
    return (row <= col) if reverse else (row >= col)


def _ssd_decays(dt_raw, dtb, a_neg, sel, reverse):
    n_chunks = dt_raw.shape[0] // CHUNK
    tri = _chunk_mask(reverse).astype(F32)
    end = 0 if reverse else CHUNK - 1
    xpre = dt_raw + dtb
    dtv = jnp.maximum(xpre, 0.0) + jnp.log1p(jnp.exp(-jnp.abs(xpre)))
    a = dtv * a_neg
    cs_chunks = [jnp.dot(tri, a[c * CHUNK:(c + 1) * CHUNK], preferred_element_type=F32,
                         precision=lax.Precision.HIGHEST) for c in range(n_chunks)]
    cs = jnp.concatenate(cs_chunks, axis=0)
    cs_end = jnp.concatenate([jnp.broadcast_to(c[end:end + 1, :], (CHUNK, LANES)) for c in cs_chunks], axis=0)
    e3 = _expand_heads(jnp.exp(cs), sel)
    e2 = _expand_heads(dtv * jnp.exp(cs_end - cs), sel)
    src_t = [(c - jnp.log(dtv[i * CHUNK:(i + 1) * CHUNK])).T for i, c in enumerate(cs_chunks)]
    return cs_chunks, src_t, e2, e3


def _ssd_chunk(xc, cs, src_t, e2, e3, h_ref, reverse):
    off = SSM_HEADS if reverse else 0
    mask = _chunk_mask(reverse)
    end = 0 if reverse else CHUNK - 1
    xs = xc[:, :SSM_INNER]
    xdec = (xs.astype(F32) * e2).astype(BF16)
    lane = lax.broadcasted_iota(jnp.int32, (CHUNK, LANES), 1)
    low = lane < SSM_HEAD_DIM
    zero = jnp.zeros((CHUNK, LANES), BF16)
    gw = SSM_INNER // SSM_GROUPS
    hpg = SSM_HEADS // SSM_GROUPS
    y_tiles = []
    for g in range(SSM_GROUPS):
        bm = xc[:, SSM_INNER + g * SSM_STATE:SSM_INNER + (g + 1) * SSM_STATE]
        cm = xc[:, SSM_INNER + (SSM_GROUPS + g) * SSM_STATE:SSM_INNER + (SSM_GROUPS + g + 1) * SSM_STATE]
        cb = _dot_nt(cm, bm)
        h_prev = h_ref[:, g * gw:(g + 1) * gw]
        y_off = _dot(cm, h_prev.astype(BF16)) * e3[:, g * gw:(g + 1) * gw]
        for jj in range(hpg // 2):
            j = g * (hpg // 2) + jj
            x_tile = xs[:, j * LANES:(j + 1) * LANES]
            acc = y_off[:, jj * LANES:(jj + 1) * LANES]
            for half in range(2):
                hd = off + 2 * j + half
                seg = cs[:, hd:hd + 1] - src_t[hd:hd + 1, :]
                m = cb * jnp.exp(jnp.where(mask, seg, -jnp.inf))
                x_half = jnp.where(low, x_tile, zero) if half == 0 else jnp.where(low, zero, x_tile)
                acc = acc + _dot(m.astype(BF16), x_half)
            y_tiles.append(acc)
        s_new = _dot_tn(bm, xdec[:, g * gw:(g + 1) * gw])
        h_ref[:, g * gw:(g + 1) * gw] = h_prev * e3[end:end + 1, g * gw:(g + 1) * gw] + s_new
    return jnp.concatenate(y_tiles, axis=1)


def _ssd_kernel(xf_ref, xb_ref, dtf_ref, dtb_ref, bias_ref, alog_ref, dskip_ref, sel_ref,
                yf_ref, yb_ref, hf_ref, hb_ref):
    c = pl.program_id(1)

    @pl.when(c == 0)
    def _():
        hf_ref[...] = jnp.zeros_like(hf_ref)
        hb_ref[...] = jnp.zeros_like(hb_ref)

    a_neg = -jnp.exp(alog_ref[...])
    bias = bias_ref[...]
    n_chunks = xf_ref.shape[1] // CHUNK
    dec_f = _ssd_decays(dtf_ref[0], bias, a_neg, sel_ref[0], reverse=False)
    dec_b = _ssd_decays(dtb_ref[0], bias, a_neg, sel_ref[1], reverse=True)

    def chunk(x_ref, dec, h_ref, c, reverse):
        cs, src_t, e2, e3 = dec
        rows = slice(c * CHUNK, (c + 1) * CHUNK)
        xc = x_ref[0, rows, :]
        return xc, _ssd_chunk(xc, cs[c], src_t[c], e2[rows], e3[rows], h_ref, reverse)

    for c in range(n_chunks):
        xf, yf = chunk(xf_ref, dec_f, hf_ref, c, False)
        yf_ref[0, c * CHUNK:(c + 1) * CHUNK, :] = (
            yf + xf[:, :SSM_INNER].astype(F32) * dskip_ref[...]).astype(BF16)
        cb = n_chunks - 1 - c
        _, yb = chunk(xb_ref, dec_b, hb_ref, cb, True)
        yb_ref[0, cb * CHUNK:(cb + 1) * CHUNK, :] = yb.astype(BF16)


def _ssd(xc, dt, bias, alog, dskip, n_chunks):
    b, l, _ = xc.shape
    rows = n_chunks * CHUNK
    nc = l // rows
    fwd = lambda n: pl.BlockSpec((1, rows, n), lambda bi, c: (bi, c, 0))
    bwd = lambda n: pl.BlockSpec((1, rows, n), lambda bi, c: (bi, nc - 1 - c, 0))
    lane_head = np.arange(SSM_INNER) // SSM_HEAD_DIM
    sel = np.stack([np.arange(DT_PAD)[:, None] == (d * SSM_HEADS + lane_head)[None, :] for d in range(2)])
    sel = jnp.asarray(np.concatenate([sel, sel], axis=1), BF16)
    return pl.pallas_call(
        _ssd_kernel,
        grid=(b, nc),
        in_specs=[fwd(CONV_DIM), bwd(CONV_DIM), fwd(DT_PAD), bwd(DT_PAD),
                  _const_spec(bias.shape), _const_spec(alog.shape), _const_spec(dskip.shape),
                  _const_spec(sel.shape)],
        out_specs=(fwd(SSM_INNER), bwd(SSM_INNER)),
        out_shape=(jax.ShapeDtypeStruct((b, l, SSM_INNER), BF16),
                   jax.ShapeDtypeStruct((b, l, SSM_INNER), BF16)),
        scratch_shapes=[pltpu.VMEM((SSM_STATE, SSM_INNER), F32),
                        pltpu.VMEM((SSM_STATE, SSM_INNER), F32)],
        compiler_params=_params("parallel", "arbitrary"),
        name="ssd",
    )(xc, xc, dt, dt, bias, alog, dskip, sel)


def _attn_kernel(bounded_ref, q_ref, k_ref, vt_ref, o_ref, m_ref, l_ref, acc_ref, *, tk):
    nk = k_ref.shape[1] // tk
    l_ref[...] = jnp.zeros_like(l_ref)
    acc_ref[...] = jnp.zeros_like(acc_ref)

    def tiles(i):
        ks = pl.multiple_of(i * tk, tk)
        return k_ref[0, pl.ds(ks, tk), :], vt_ref[0, :, pl.ds(ks, tk)]

    def bounded_body(i, carry):
        k, vt = tiles(i)
        for j in range(ATTN_GROUP):
            rows = slice(j * ATTN_HEAD_DIM, (j + 1) * ATTN_HEAD_DIM)
            p_t = jnp.exp2(_dot_nt(k, q_ref[0, :, j * LANES:(j + 1) * LANES]))
            part = p_t[0:SUBLANES, :]
            for r in range(1, tk // SUBLANES):
                part = part + p_t[r * SUBLANES:(r + 1) * SUBLANES, :]
            l_ref[j * SUBLANES:(j + 1) * SUBLANES, :] += part
            acc_ref[rows, :] += _dot(vt, p_t.astype(BF16))
        return carry

    def online_body(i, carry):
        k, vt = tiles(i)
        for j in range(ATTN_GROUP):
            rows = slice(j * ATTN_HEAD_DIM, (j + 1) * ATTN_HEAD_DIM)
            s_t = _dot_nt(k, q_ref[0, :, j * LANES:(j + 1) * LANES])
            m_prev = m_ref[j:j + 1, :]
            m_new = jnp.maximum(m_prev, jnp.max(s_t, axis=0, keepdims=True))
            alpha = jnp.exp2(m_prev - m_new)
            p_t = jnp.exp2(s_t - m_new)
            lrow = slice(j * SUBLANES, j * SUBLANES + 1)
            l_ref[lrow, :] = alpha * l_ref[lrow, :] + jnp.sum(p_t, axis=0, keepdims=True)
            acc_ref[rows, :] = alpha * acc_ref[rows, :] + _dot(vt, p_t.astype(BF16))
            m_ref[j:j + 1, :] = m_new
        return carry

    bounded = bounded_ref[0] != 0

    @pl.when(bounded)
    def _():
        lax.fori_loop(0, nk, bounded_body, 0)

    @pl.when(jnp.logical_not(bounded))
    def _():
        m_ref[...] = jnp.full_like(m_ref, -jnp.inf)
        lax.fori_loop(0, nk, online_body, 0)

    for j in range(ATTN_GROUP):
        rows = slice(j * ATTN_HEAD_DIM, (j + 1) * ATTN_HEAD_DIM)
        l = jnp.sum(l_ref[j * SUBLANES:(j + 1) * SUBLANES, :], axis=0, keepdims=True)
        acc_ref[rows, :] = acc_ref[rows, :] / l
    o_ref[0] = acc_ref[...].T


def _attention(bounded, qp, kp, vt, tq, tk):
    b, l, _ = qp.shape
    gw = ATTN_GROUP * LANES
    return pl.pallas_call(
        functools.partial(_attn_kernel, tk=tk),
        grid=(b, ATTN_KV_HEADS, l // tq),
        in_specs=[pl.BlockSpec(memory_space=pltpu.SMEM),
                  pl.BlockSpec((1, tq, gw), lambda bi, g, i: (bi, i, g)),
                  pl.BlockSpec((1, l, LANES), lambda bi, g, i: (bi, 0, 0)),
                  pl.BlockSpec((1, ATTN_HEAD_DIM, l), lambda bi, g, i: (bi, g, 0))],
        out_specs=pl.BlockSpec((1, tq, ATTN_GROUP * ATTN_HEAD_DIM), lambda bi, g, i: (bi, i, g)),
        out_shape=jax.ShapeDtypeStruct((b, l, ATTN_INNER), F32),
        scratch_shapes=[pltpu.VMEM((SUBLANES, tq), F32), pltpu.VMEM((ATTN_GROUP * SUBLANES, tq), F32),
                        pltpu.VMEM((ATTN_GROUP * ATTN_HEAD_DIM, tq), F32)],
        compiler_params=_params("parallel", "parallel", "parallel"),
        name="attention",
    )(bounded, qp, kp, vt)


def _mem_kv_kernel(mem_ref, g_ref, wk_ref, wv_ref, k_ref, v_ref):
    m = _rmsnorm(mem_ref[0], g_ref[...]).astype(BF16)
    k_ref[0] = _dot(m, wk_ref[...]).astype(BF16)
    v_ref[0] = _dot(m, wv_ref[...]).astype(BF16)


def _mem_kv(mem, g, wk, wv):
    b, n, d = mem.shape
    spec = pl.BlockSpec((1, n, X_INNER), lambda bi: (bi, 0, 0))
    return pl.pallas_call(
        _mem_kv_kernel,
        grid=(b,),
        in_specs=[pl.BlockSpec((1, n, d), lambda bi: (bi, 0, 0)), _const_spec(g.shape),
                  _const_spec(wk.shape), _const_spec(wv.shape)],
        out_specs=(spec, spec),
        out_shape=(jax.ShapeDtypeStruct((b, n, X_INNER), BF16),) * 2,
        compiler_params=_params("parallel"),
        name="mem_kv",
    )(mem, g, wk, wv)


def _mix_xattn_kernel(x_ref, yf_ref, yb_ref, z_ref, o_ref, kx_ref, vx_ref,
                      gssm_ref, gatt_ref, wos_ref, woa_ref, gpost_ref,
                      gxpre_ref, wxq_ref, wxo_ref, gxpost_ref, out_ref, *, n_sub):
    tm = x_ref.shape[1]
    sub = tm // n_sub
    gw = SSM_INNER // SSM_GROUPS
    scale = X_HEAD_DIM ** -0.5
    for r in range(n_sub):
        rows = slice(r * sub, (r + 1) * sub)
        z = z_ref[0, rows, :].astype(F32)
        y = (yf_ref[0, rows, :].astype(F32) + yb_ref[0, rows, :].astype(F32)) * (z * _sigmoid(z))
        yn = jnp.concatenate(
            [y[:, g * gw:(g + 1) * gw]
             * lax.rsqrt(jnp.mean(y[:, g * gw:(g + 1) * gw] * y[:, g * gw:(g + 1) * gw], axis=-1, keepdims=True)
                         + EPS)
             for g in range(SSM_GROUPS)], axis=1)
        y_ssm = (yn * gssm_ref[...]).astype(BF16)
        y_att = _rmsnorm(o_ref[0, rows, :], gatt_ref[...]).astype(BF16)
        mix = _dot(y_ssm, wos_ref[...]) + _dot(y_att, woa_ref[...])
        x1 = x_ref[0, rows, :] + _rmsnorm(mix, gpost_ref[...])

        h = _rmsnorm(x1, gxpre_ref[...]).astype(BF16)
        q = _dot(h, wxq_ref[...])
        heads = []
        for hh in range(X_HEADS):
            cols = slice(hh * X_HEAD_DIM, (hh + 1) * X_HEAD_DIM)
            s = _dot_nt(q[:, cols].astype(BF16), kx_ref[0, :, cols]) * scale
            e = jnp.exp(s - jnp.max(s, axis=-1, keepdims=True))
            p = e / jnp.sum(e, axis=-1, keepdims=True)
            heads.append(_dot(p.astype(BF16), vx_ref[0, :, cols]))
        ca = _dot(jnp.concatenate(heads, axis=1).astype(BF16), wxo_ref[...])
        out_ref[0, rows, :] = x1 + _rmsnorm(ca, gxpost_ref[...])


def _mix_xattn(x, yf, yb, z, o, kx, vx, gssm, gatt, wos, woa, gpost, gxpre, wxq, wxo, gxpost, tm):
    b, l, d = x.shape
    n_mem = kx.shape[1]
    tok = lambda n: pl.BlockSpec((1, tm, n), lambda bi, i: (bi, i, 0))
    mem = pl.BlockSpec((1, n_mem, X_INNER), lambda bi, i: (bi, 0, 0))
    consts = (gssm, gatt, wos, woa, gpost, gxpre, wxq, wxo, gxpost)
    return pl.pallas_call(
        functools.partial(_mix_xattn_kernel, n_sub=1),
        grid=(b, l // tm),
        in_specs=[tok(d), tok(SSM_INNER), tok(SSM_INNER), tok(SSM_INNER), tok(ATTN_INNER), mem, mem]
                 + [_const_spec(c.shape) for c in consts],
        out_specs=tok(d),
        out_shape=jax.ShapeDtypeStruct((b, l, d), F32),
        compiler_params=_params("parallel", "parallel"),
        name="mix_xattn",
    )(x, yf, yb, z, o, kx, vx, *consts)


def _ffn_kernel(x_ref, gpre_ref, wg_ref, wu_ref, wd_ref, gpost_ref, out_ref, *, n_split):
    x = x_ref[0]
    h = _rmsnorm(x, gpre_ref[...]).astype(BF16)
    d_ff = wg_ref.shape[1]
    w = d_ff // n_split
    f = None
    for c in range(n_split):
        cols = slice(c * w, (c + 1) * w)
        gt = _dot(h, wg_ref[:, cols])
        up = _dot(h, wu_ref[:, cols])
        part = _dot((gt * _sigmoid(gt) * up).astype(BF16), wd_ref[cols, :])
        f = part if f is None else f + part
    out_ref[0] = x + _rmsnorm(f, gpost_ref[...])


def _ffn(x, gpre, wg, wu, wd, gpost, tm, n_split):
    b, l, d = x.shape
    tok = pl.BlockSpec((1, tm, d), lambda bi, i: (bi, i, 0))
    single = lambda a: pl.BlockSpec(a.shape, lambda *_: (0,) * a.ndim, pipeline_mode=pl.Buffered(1))
    return pl.pallas_call(
        functools.partial(_ffn_kernel, n_split=n_split),
        grid=(b, l // tm),
        in_specs=[tok, _const_spec(gpre.shape), single(wg), single(wu), single(wd), _const_spec(gpost.shape)],
        out_specs=tok,
        out_shape=jax.ShapeDtypeStruct((b, l, d), F32),
        compiler_params=_params("parallel", "parallel"),
        name="ffn",
    )(x, gpre, wg, wu, wd, gpost)


def _rope_tables(l):
    t = jnp.arange(l, dtype=jnp.int32)
    r = (t // GRID_W).astype(F32)
    c = (t % GRID_W).astype(F32)
    half = ATTN_HEAD_DIM // 2
    inv = 1.0 / (ROPE_THETA ** (jnp.arange(0, half, 2, dtype=F32) / half))
    ang_r = r[:, None] * inv
    ang_c = c[:, None] * inv
    cos = jnp.concatenate([jnp.cos(ang_r)] * 2 + [jnp.cos(ang_c)] * 2, axis=1)
    sin = jnp.concatenate([-jnp.sin(ang_r), jnp.sin(ang_r), -jnp.sin(ang_c), jnp.sin(ang_c)], axis=1)
    return jnp.tile(cos, (1, 2)), jnp.tile(sin, (1, 2))


def _tile(l, want):
    t = min(l, want)
    assert l % t == 0
    return t


def _layer(x, mem, w):
    b, l, _ = x.shape
    assert l % CHUNK == 0 and l % GRID_W == 0
    tm = _tile(l, 512)
    cos, sin = _rope_tables(l)
    z, xc, dt, qp, kp, vt = _in_proj(x, w["g_pre"], w["wz"], w["wxbc"], w["wdt"], w["wq"], w["wk"], w["wv"],
                                     cos, sin, w["qg"], w["kg"], w["conv_w"], w["conv_b"], tm)
    yf, yb = _ssd(xc, dt, w["dt_bias"], w["a_log"], w["d_skip"], _tile(l, 4 * CHUNK) // CHUNK)
    o = _attention(w["attn_bounded"], qp, kp, vt, _tile(l, 1024), _tile(l, 1024))
    kx, vx = _mem_kv(mem, w["g_mem"], w["wxk"], w["wxv"])
    x2 = _mix_xattn(x, yf, yb, z, o, kx, vx, w["g_ssm"], w["g_att"], w["wos"], w["woa"], w["g_post"],
                    w["g_xpre"], w["wxq"], w["wxo"], w["g_xpost"], tm)
    return _ffn(x2, w["g_fpre"], w["wg"], w["wu"], w["wd"], w["g_fpost"], tm, 2)


def _prep_weights(i, norm_mix_pre, w_in, conv_w, conv_b, dt_bias, a_log, d_skip, ssm_norm, q_norm, k_norm,
                  attn_norm, w_out, norm_mix_post, norm_x_pre, norm_mem, w_xq, w_xk, w_xv, w_xo, norm_x_post,
                  norm_ffn_pre, w_gate, w_up, w_down, norm_ffn_post):
    row = lambda v: v.reshape(1, -1).astype(F32)
    splits = np.cumsum([SSM_INNER, CONV_DIM, 2 * SSM_HEADS, ATTN_INNER,
                        ATTN_KV_HEADS * ATTN_HEAD_DIM, ATTN_KV_HEADS * ATTN_HEAD_DIM])[:-1]
    wz, wxbc, wdt, wq, wk, wv = jnp.split(w_in[i], [int(s) for s in splits], axis=-1)
    order = [h + ATTN_GROUP * g for h in range(ATTN_GROUP) for g in range(ATTN_KV_HEADS)]
    wq = wq.reshape(D_MODEL, ATTN_HEADS, ATTN_HEAD_DIM)[:, order, :].reshape(D_MODEL, ATTN_INNER)
    pad = DT_PAD - 2 * SSM_HEADS
    q_gain = row(q_norm[i]) * (ATTN_HEAD_DIM ** -0.5 * np.log2(np.e))
    k_gain = row(k_norm[i])
    score_bound = ATTN_HEAD_DIM * jnp.max(jnp.abs(q_gain)) * jnp.max(jnp.abs(k_gain))
    return {
        "attn_bounded": (score_bound * BOUND_MARGIN <= MAX_EXP2_SCORE).astype(jnp.int32).reshape(1),
        "g_pre": row(norm_mix_pre[i]),
        "wz": wz.astype(BF16), "wxbc": wxbc.astype(BF16),
        "wdt": jnp.pad(wdt, ((0, 0), (0, pad))).astype(BF16),
        "wq": wq.astype(BF16), "wk": wk.astype(BF16), "wv": wv.astype(BF16),
        "qg": jnp.tile(q_gain, (1, LANES // ATTN_HEAD_DIM)),
        "kg": jnp.tile(k_gain, (1, LANES // ATTN_HEAD_DIM)),
        "conv_w": conv_w[i].astype(F32), "conv_b": row(conv_b[i]),
        "dt_bias": jnp.pad(row(dt_bias[i]), ((0, 0), (0, pad))),
        "a_log": jnp.pad(row(a_log[i]), ((0, 0), (0, pad))),
        "d_skip": jnp.repeat(row(d_skip[i]), SSM_HEAD_DIM, axis=1),
        "g_ssm": row(ssm_norm[i]), "g_att": row(attn_norm[i]),
        "wos": w_out[i][:SSM_INNER].astype(BF16), "woa": w_out[i][SSM_INNER:].astype(BF16),
        "g_post": row(norm_mix_post[i]), "g_xpre": row(norm_x_pre[i]), "g_mem": row(norm_mem[i]),
        "wxq": w_xq[i].astype(BF16), "wxk": w_xk[i].astype(BF16), "wxv": w_xv[i].astype(BF16),
        "wxo": w_xo[i].astype(BF16), "g_xpost": row(norm_x_post[i]),
        "g_fpre": row(norm_ffn_pre[i]), "wg": w_gate[i].astype(BF16), "wu": w_up[i].astype(BF16),
        "wd": w_down[i].astype(BF16), "g_fpost": row(norm_ffn_post[i]),
    }


def kernel(x_prompt, x_sample, mem_prompt, mem_sample, norm_mix_pre, w_in, conv_w, conv_b, dt_bias, a_log,
           d_skip, ssm_norm, q_norm, k_norm, attn_norm, w_out, norm_mix_post, norm_x_pre, norm_mem, w_xq,
           w_xk, w_xv, w_xo, norm_x_post, norm_ffn_pre, w_gate, w_up, w_down, norm_ffn_post):
    y_prompt, y_sample = x_prompt, x_sample
    for i in range(w_in.shape[0]):
        w = _prep_weights(i, norm_mix_pre, w_in, conv_w, conv_b, dt_bias, a_log, d_skip, ssm_norm, q_norm,
                          k_norm, attn_norm, w_out, norm_mix_post, norm_x_pre, norm_mem, w_xq, w_xk, w_xv,
                          w_xo, norm_x_post, norm_ffn_pre, w_gate, w_up, w_down, norm_ffn_post)
        y_prompt = _layer(y_prompt, mem_prompt, w)
        y_sample = _layer(y_sample, mem_sample, w)
    return (y_prompt, y_sample)
```

```python
import functools

import numpy as np
import jax
import jax.numpy as jnp
from jax import lax
from jax.experimental import pallas as pl
from jax.experimental.pallas import tpu as pltpu

F32 = jnp.float32
BF16 = jnp.bfloat16

EPS = 1e-6
LANES = 128
SUBLANES = 8
VMEM_LIMIT_BYTES = 56 * 1024 * 1024

D_MODEL = 1024
SSM_HEADS = 16
SSM_HEAD_DIM = 64
SSM_INNER = SSM_HEADS * SSM_HEAD_DIM
SSM_GROUPS = 2
SSM_STATE = 128
D_CONV = 5
CHUNK = 128
CONV_DIM = SSM_INNER + 2 * SSM_GROUPS * SSM_STATE
ATTN_HEADS = 8
ATTN_KV_HEADS = 2
ATTN_HEAD_DIM = 64
ATTN_INNER = ATTN_HEADS * ATTN_HEAD_DIM
ATTN_GROUP = ATTN_HEADS // ATTN_KV_HEADS
GRID_W = 64
ROPE_THETA = 10000.0
X_HEADS = 4
X_HEAD_DIM = 128
X_INNER = X_HEADS * X_HEAD_DIM
DT_PAD = LANES
HALO = SUBLANES
MAX_EXP2_SCORE = 60.0
BOUND_MARGIN = 1.02


def _dot(a, b):
    return jnp.dot(a, b, preferred_element_type=F32)


def _dot_nt(a, b):
    return lax.dot_general(a, b, (((1,), (1,)), ((), ())), preferred_element_type=F32)


def _dot_tn(a, b):
    return lax.dot_general(a, b, (((0,), (0,)), ((), ())), preferred_element_type=F32)


def _rmsnorm(x, g):
    return x * lax.rsqrt(jnp.mean(x * x, axis=-1, keepdims=True) + EPS) * g


def _sigmoid(x):
    return 1.0 / (1.0 + jnp.exp(-x))


def _params(*sem):
    return pltpu.CompilerParams(dimension_semantics=sem, vmem_limit_bytes=VMEM_LIMIT_BYTES)


def _const_spec(shape):
    nd = len(shape)
    return pl.BlockSpec(shape, lambda *_: (0,) * nd)


def _in_proj_kernel(x_ref, xp_ref, xn_ref, g_ref, wxbc_ref, wrest_ref,
                    cos_ref, sin_ref, qg_ref, kg_ref, cw_ref, cb_ref,
                    z_ref, xc_ref, dt_ref, qp_ref, kp_ref, vt_ref, ext_ref, conv_ref):
    i = pl.program_id(1)
    n = pl.num_programs(1)
    tm = x_ref.shape[1]
    x_all = jnp.concatenate([x_ref[0], xp_ref[0], xn_ref[0]], axis=0)
    h_all = _rmsnorm(x_all, g_ref[...]).astype(BF16)
    h = h_all[:tm]

    xbc = _dot(h_all, wxbc_ref[...])
    for s in range(CONV_DIM // LANES):
        cols = slice(s * LANES, (s + 1) * LANES)
        ext_ref[s, 0:HALO, :] = jnp.where(i > 0, xbc[tm:tm + HALO, cols], 0.0)
        ext_ref[s, HALO:HALO + tm, :] = xbc[:tm, cols]
        ext_ref[s, HALO + tm:HALO + tm + HALO, :] = jnp.where(i < n - 1, xbc[tm + HALO:, cols], 0.0)
        ext_ref[s, HALO + tm + HALO:, :] = jnp.zeros((HALO, LANES), F32)
    pitch = tm // SUBLANES + 1
    for s in range(CONV_DIM // LANES):
        cols = slice(s * LANES, (s + 1) * LANES)
        taps = [cw_ref[j:j + 1, cols] for j in range(D_CONV)]
        bias = cb_ref[:, cols]
        for a in range(pitch):
            acc = bias
            for j in range(D_CONV):
                start = HALO - D_CONV // 2 + j + a
                acc = acc + ext_ref[s, pl.ds(start, SUBLANES, stride=pitch), :] * taps[j]
            conv_ref[s, pl.ds(a, SUBLANES, stride=pitch), :] = acc * _sigmoid(acc)
        xc_ref[0, :, cols] = conv_ref[s, 0:tm, :].astype(BF16)

    rest = _dot(h, wrest_ref[...])
    z_ref[0] = rest[:, :SSM_INNER].astype(BF16)
    q = rest[:, SSM_INNER:SSM_INNER + ATTN_INNER]
    k = rest[:, SSM_INNER + ATTN_INNER:SSM_INNER + ATTN_INNER + LANES]
    v = rest[:, SSM_INNER + ATTN_INNER + LANES:SSM_INNER + ATTN_INNER + 2 * LANES]
    dt_ref[0] = rest[:, SSM_INNER + ATTN_INNER + 2 * LANES:]
    vt_ref[0] = v.T.astype(BF16)

    cos = cos_ref[...]
    sin = sin_ref[...]
    lane = lax.broadcasted_iota(jnp.int32, cos.shape, 1)

    r_id = (lax.broadcasted_iota(jnp.int32, (2 * LANES, LANES), 0) % LANES) // ATTN_HEAD_DIM
    c_id = lax.broadcasted_iota(jnp.int32, (2 * LANES, LANES), 1) // ATTN_HEAD_DIM
    same_head = jnp.where(r_id == c_id, 1.0, 0.0).astype(BF16)

    def head_norm_rope(blk, gain):
        sq = blk * blk
        hi = sq.astype(BF16)
        lo = (sq - hi.astype(F32)).astype(BF16)
        ss = _dot(jnp.concatenate([hi, lo], axis=1), same_head)
        y = blk * lax.rsqrt(ss * (1.0 / ATTN_HEAD_DIM) + EPS) * gain
        partner = jnp.where((lane & 16) != 0, pltpu.roll(y, 16, 1), pltpu.roll(y, LANES - 16, 1))
        return y * cos + partner * sin

    kp_ref[0] = head_norm_rope(k, kg_ref[...]).astype(BF16)
    low = lane < ATTN_HEAD_DIM
    qg = qg_ref[...]
    for j in range(ATTN_GROUP):
        r = head_norm_rope(q[:, j * LANES:(j + 1) * LANES], qg)
        qp_ref[0, :, j * LANES:(j + 1) * LANES] = jnp.where(low, r, 0.0).astype(BF16)
        qp_ref[0, :, (ATTN_GROUP + j) * LANES:(ATTN_GROUP + j + 1) * LANES] = jnp.where(low, 0.0, r).astype(BF16)


def _in_proj(x, g, wxbc, wrest, cos, sin, qg, kg, cw, cb, tm):
    b, l, d = x.shape
    grid = (b, l // tm)
    nh = tm // HALO
    last = l // HALO - 1
    tok = lambda n: pl.BlockSpec((1, tm, n), lambda bi, i: (bi, i, 0))
    pos = pl.BlockSpec((tm, LANES), lambda bi, i: (i, 0))
    prev_rows = pl.BlockSpec((1, HALO, d), lambda bi, i: (bi, jnp.maximum(i * nh - 1, 0), 0))
    next_rows = pl.BlockSpec((1, HALO, d), lambda bi, i: (bi, jnp.minimum((i + 1) * nh, last), 0))
    out_shape = (
        jax.ShapeDtypeStruct((b, l, SSM_INNER), BF16),
        jax.ShapeDtypeStruct((b, l, CONV_DIM), BF16),
        jax.ShapeDtypeStruct((b, l, DT_PAD), F32),
        jax.ShapeDtypeStruct((b, l, ATTN_HEADS * LANES), BF16),
        jax.ShapeDtypeStruct((b, l, LANES), BF16),
        jax.ShapeDtypeStruct((b, LANES, l), BF16),
    )
    return pl.pallas_call(
        _in_proj_kernel,
        grid=grid,
        in_specs=[tok(d), prev_rows, next_rows, _const_spec(g.shape), _const_spec(wxbc.shape),
                  _const_spec(wrest.shape), pos, pos, _const_spec(qg.shape), _const_spec(kg.shape),
                  _const_spec(cw.shape), _const_spec(cb.shape)],
        out_specs=(tok(SSM_INNER), tok(CONV_DIM), tok(DT_PAD), tok(ATTN_HEADS * LANES), tok(LANES),
                   pl.BlockSpec((1, LANES, tm), lambda bi, i: (bi, 0, i))),
        out_shape=out_shape,
        scratch_shapes=[pltpu.VMEM((CONV_DIM // LANES, tm + 3 * HALO, LANES), F32),
                        pltpu.VMEM((CONV_DIM // LANES, tm + SUBLANES, LANES), F32)],
        compiler_params=_params("parallel", "parallel"),
        name="in_proj",
    )(x, x, x, g, wxbc, wrest, cos, sin, qg, kg, cw, cb)


def _expand_heads(w, sel):
    hi = w.astype(BF16)
    lo = (w - hi.astype(F32)).astype(BF16)
    return _dot(jnp.concatenate([hi, lo], axis=1), sel)


def _chunk_mask(reverse):
    row = lax.broadcasted_iota(jnp.int32, (CHUNK, CHUNK), 0)
    col = lax.broadcasted_iota(jnp.int32, (CHUNK, CHUNK), 1)
    return (row <= col) if reverse else (row >= col)


def _ssd_decays(dt_raw, dtb, a_neg, sel, reverse):
    n_chunks = dt_raw.shape[0] // CHUNK
    tri = _chunk_mask(reverse).astype(F32)
    end = 0 if reverse else CHUNK - 1
    xpre = dt_raw + dtb
    dtv = jnp.maximum(xpre, 0.0) + jnp.log1p(jnp.exp(-jnp.abs(xpre)))
    a = dtv * a_neg
    cs_chunks = [jnp.dot(tri, a[c * CHUNK:(c + 1) * CHUNK], preferred_element_type=F32,
                         precision=lax.Precision.HIGHEST) for c in range(n_chunks)]
    cs = jnp.concatenate(cs_chunks, axis=0)
    cs_end = jnp.concatenate([jnp.broadcast_to(c[end:end + 1, :], (CHUNK, LANES)) for c in cs_chunks], axis=0)
    e3 = _expand_heads(jnp.exp(cs), sel)
    e2 = _expand_heads(dtv * jnp.exp(cs_end - cs), sel)
    src_t = [(c - jnp.log(dtv[i * CHUNK:(i + 1) * CHUNK])).T for i, c in enumerate(cs_chunks)]
    return cs_chunks, src_t, e2, e3


def _ssd_chunk(xc, cs, src_t, e2, e3, h_ref, reverse):
    off = SSM_HEADS if reverse else 0
    mask = _chunk_mask(reverse)
    end = 0 if reverse else CHUNK - 1
    xs = xc[:, :SSM_INNER]
    xdec = (xs.astype(F32) * e2).astype(BF16)
    lane = lax.broadcasted_iota(jnp.int32, (CHUNK, LANES), 1)
    low = lane < SSM_HEAD_DIM
    zero = jnp.zeros((CHUNK, LANES), BF16)
    gw = SSM_INNER // SSM_GROUPS
    hpg = SSM_HEADS // SSM_GROUPS
    y_tiles = []
    for g in range(SSM_GROUPS):
        bm = xc[:, SSM_INNER + g * SSM_STATE:SSM_INNER + (g + 1) * SSM_STATE]
        cm = xc[:, SSM_INNER + (SSM_GROUPS + g) * SSM_STATE:SSM_INNER + (SSM_GROUPS + g + 1) * SSM_STATE]
        cb = _dot_nt(cm, bm)
        h_prev = h_ref[:, g * gw:(g + 1) * gw]
        y_off = _dot(cm, h_prev.astype(BF16)) * e3[:, g * gw:(g + 1) * gw]
        for jj in range(hpg // 2):
            j = g * (hpg // 2) + jj
            x_tile = xs[:, j * LANES:(j + 1) * LANES]
            ms = []
            for half in range(2):
                hd = off + 2 * j + half
                seg = cs[:, hd:hd + 1] - src_t[hd:hd + 1, :]
                ms.append((cb * jnp.exp(jnp.where(mask, seg, -jnp.inf))).astype(BF16))
            x_pair = jnp.concatenate([jnp.where(low, x_tile, zero), jnp.where(low, zero, x_tile)], axis=0)
            y_tiles.append(y_off[:, jj * LANES:(jj + 1) * LANES] + _dot(jnp.concatenate(ms, axis=1), x_pair))
        s_new = _dot_tn(bm, xdec[:, g * gw:(g + 1) * gw])
        h_ref[:, g * gw:(g + 1) * gw] = h_prev * e3[end:end + 1, g * gw:(g + 1) * gw] + s_new
    return jnp.concatenate(y_tiles, axis=1)


def _ssd_kernel(xf_ref, xb_ref, dtf_ref, dtb_ref, bias_ref, alog_ref, dskip_ref, sel_ref,
                yf_ref, yb_ref, hf_ref, hb_ref):
    c = pl.program_id(1)

    @pl.when(c == 0)
    def _():
        hf_ref[...] = jnp.zeros_like(hf_ref)
        hb_ref[...] = jnp.zeros_like(hb_ref)

    a_neg = -jnp.exp(alog_ref[...])
    bias = bias_ref[...]
    n_chunks = xf_ref.shape[1] // CHUNK
    dec_f = _ssd_decays(dtf_ref[0], bias, a_neg, sel_ref[0], reverse=False)
    dec_b = _ssd_decays(dtb_ref[0], bias, a_neg, sel_ref[1], reverse=True)

    def chunk(x_ref, dec, h_ref, c, reverse):
        cs, src_t, e2, e3 = dec
        rows = slice(c * CHUNK, (c + 1) * CHUNK)
        xc = x_ref[0, rows, :]
        return xc, _ssd_chunk(xc, cs[c], src_t[c], e2[rows], e3[rows], h_ref, reverse)

    for c in range(n_chunks):
        xf, yf = chunk(xf_ref, dec_f, hf_ref, c, False)
        yf_ref[0, c * CHUNK:(c + 1) * CHUNK, :] = (
            yf + xf[:, :SSM_INNER].astype(F32) * dskip_ref[...]).astype(BF16)
        cb = n_chunks - 1 - c
        _, yb = chunk(xb_ref, dec_b, hb_ref, cb, True)
        yb_ref[0, cb * CHUNK:(cb + 1) * CHUNK, :] = yb.astype(BF16)


def _ssd(xc, dt, bias, alog, dskip, n_chunks):
    b, l, _ = xc.shape
    rows = n_chunks * CHUNK
    nc = l // rows
    fwd = lambda n: pl.BlockSpec((1, rows, n), lambda bi, c: (bi, c, 0))
    bwd = lambda n: pl.BlockSpec((1, rows, n), lambda bi, c: (bi, nc - 1 - c, 0))
    lane_head = np.arange(SSM_INNER) // SSM_HEAD_DIM
    sel = np.stack([np.arange(DT_PAD)[:, None] == (d * SSM_HEADS + lane_head)[None, :] for d in range(2)])
    sel = jnp.asarray(np.concatenate([sel, sel], axis=1), BF16)
    return pl.pallas_call(
        _ssd_kernel,
        grid=(b, nc),
        in_specs=[fwd(CONV_DIM), bwd(CONV_DIM), fwd(DT_PAD), bwd(DT_PAD),
                  _const_spec(bias.shape), _const_spec(alog.shape), _const_spec(dskip.shape),
                  _const_spec(sel.shape)],
        out_specs=(fwd(SSM_INNER), bwd(SSM_INNER)),
        out_shape=(jax.ShapeDtypeStruct((b, l, SSM_INNER), BF16),
                   jax.ShapeDtypeStruct((b, l, SSM_INNER), BF16)),
        scratch_shapes=[pltpu.VMEM((SSM_STATE, SSM_INNER), F32),
                        pltpu.VMEM((SSM_STATE, SSM_INNER), F32)],
        compiler_params=_params("parallel", "arbitrary"),
        name="ssd",
    )(xc, xc, dt, dt, bias, alog, dskip, sel)


def _attn_kernel(bounded_ref, q_ref, k_ref, vt_ref, o_ref, m_ref, l_ref, acc_ref, *, tk):
    nk = k_ref.shape[1] // tk
    l_ref[...] = jnp.zeros_like(l_ref)
    acc_ref[...] = jnp.zeros_like(acc_ref)

    def tiles(i):
        ks = pl.multiple_of(i * tk, tk)
        return k_ref[0, pl.ds(ks, tk), :], vt_ref[0, :, pl.ds(ks, tk)]

    def bounded_body(i, carry):
        k, vt = tiles(i)
        for j in range(ATTN_GROUP):
            rows = slice(j * ATTN_HEAD_DIM, (j + 1) * ATTN_HEAD_DIM)
            p_t = jnp.exp2(_dot_nt(k, q_ref[0, :, j * LANES:(j + 1) * LANES]))
            part = p_t[0:SUBLANES, :]
            for r in range(1, tk // SUBLANES):
                part = part + p_t[r * SUBLANES:(r + 1) * SUBLANES, :]
            l_ref[j * SUBLANES:(j + 1) * SUBLANES, :] += part
            acc_ref[rows, :] += _dot(vt, p_t.astype(BF16))
        return carry

    def online_body(i, carry):
        k, vt = tiles(i)
        for j in range(ATTN_GROUP):
            rows = slice(j * ATTN_HEAD_DIM, (j + 1) * ATTN_HEAD_DIM)
            s_t = _dot_nt(k, q_ref[0, :, j * LANES:(j + 1) * LANES])
            m_prev = m_ref[j:j + 1, :]
            m_new = jnp.maximum(m_prev, jnp.max(s_t, axis=0, keepdims=True))
            alpha = jnp.exp2(m_prev - m_new)
            p_t = jnp.exp2(s_t - m_new)
            lrow = slice(j * SUBLANES, j * SUBLANES + 1)
            l_ref[lrow, :] = alpha * l_ref[lrow, :] + jnp.sum(p_t, axis=0, keepdims=True)
            acc_ref[rows, :] = alpha * acc_ref[rows, :] + _dot(vt, p_t.astype(BF16))
            m_ref[j:j + 1, :] = m_new
        return carry

    bounded = bounded_ref[0] != 0

    @pl.when(bounded)
    def _():
        lax.fori_loop(0, nk, bounded_body, 0)

    @pl.when(jnp.logical_not(bounded))
    def _():
        m_ref[...] = jnp.full_like(m_ref, -jnp.inf)
        lax.fori_loop(0, nk, online_body, 0)

    for j in range(ATTN_GROUP):
        rows = slice(j * ATTN_HEAD_DIM, (j + 1) * ATTN_HEAD_DIM)
        l = jnp.sum(l_ref[j * SUBLANES:(j + 1) * SUBLANES, :], axis=0, keepdims=True)
        acc_ref[rows, :] = acc_ref[rows, :] / l
    o_ref[0] = acc_ref[...].T


def _attention(bounded, qp, kp, vt, tq, tk):
    b, l, _ = qp.shape
    gw = ATTN_GROUP * LANES
    return pl.pallas_call(
        functools.partial(_attn_kernel, tk=tk),
        grid=(b, ATTN_KV_HEADS, l // tq),
        in_specs=[pl.BlockSpec(memory_space=pltpu.SMEM),
                  pl.BlockSpec((1, tq, gw), lambda bi, g, i: (bi, i, g)),
                  pl.BlockSpec((1, l, LANES), lambda bi, g, i: (bi, 0, 0)),
                  pl.BlockSpec((1, ATTN_HEAD_DIM, l), lambda bi, g, i: (bi, g, 0))],
        out_specs=pl.BlockSpec((1, tq, ATTN_GROUP * ATTN_HEAD_DIM), lambda bi, g, i: (bi, i, g)),
        out_shape=jax.ShapeDtypeStruct((b, l, ATTN_INNER), F32),
        scratch_shapes=[pltpu.VMEM((SUBLANES, tq), F32), pltpu.VMEM((ATTN_GROUP * SUBLANES, tq), F32),
                        pltpu.VMEM((ATTN_GROUP * ATTN_HEAD_DIM, tq), F32)],
        compiler_params=_params("parallel", "parallel", "parallel"),
        name="attention",
    )(bounded, qp, kp, vt)


def _mem_kv_kernel(mem_ref, g_ref, wk_ref, wv_ref, k_ref, v_ref):
    m = _rmsnorm(mem_ref[0], g_ref[...]).astype(BF16)
    k_ref[0] = _dot(m, wk_ref[...]).astype(BF16)
    v_ref[0] = _dot(m, wv_ref[...]).astype(BF16)


def _mem_kv(mem, g, wk, wv):
    b, n, d = mem.shape
    spec = pl.BlockSpec((1, n, X_INNER), lambda bi: (bi, 0, 0))
    return pl.pallas_call(
        _mem_kv_kernel,
        grid=(b,),
        in_specs=[pl.BlockSpec((1, n, d), lambda bi: (bi, 0, 0)), _const_spec(g.shape),
                  _const_spec(wk.shape), _const_spec(wv.shape)],
        out_specs=(spec, spec),
        out_shape=(jax.ShapeDtypeStruct((b, n, X_INNER), BF16),) * 2,
        compiler_params=_params("parallel"),
        name="mem_kv",
    )(mem, g, wk, wv)


def _mix_xattn_kernel(x_ref, yf_ref, yb_ref, z_ref, o_ref, kx_ref, vx_ref,
                      gssm_ref, gatt_ref, wout_ref, gpost_ref,
                      gxpre_ref, wxq_ref, wxo_ref, gxpost_ref, out_ref):
    tm = x_ref.shape[1]
    gw = SSM_INNER // SSM_GROUPS
    scale = X_HEAD_DIM ** -0.5

    def gate_norm(rows):
        z = z_ref[0, rows, :].astype(F32)
        y = (yf_ref[0, rows, :].astype(F32) + yb_ref[0, rows, :].astype(F32)) * (z * _sigmoid(z))
        yn = jnp.concatenate(
            [y[:, g * gw:(g + 1) * gw]
             * lax.rsqrt(jnp.mean(y[:, g * gw:(g + 1) * gw] * y[:, g * gw:(g + 1) * gw], axis=-1, keepdims=True)
                         + EPS)
             for g in range(SSM_GROUPS)], axis=1)
        return (yn * gssm_ref[...]).astype(BF16), _rmsnorm(o_ref[0, rows, :], gatt_ref[...]).astype(BF16)

    def mix_proj(ys):
        return _dot(ys[0], wout_ref[:SSM_INNER, :]) + _dot(ys[1], wout_ref[SSM_INNER:, :])

    def residual_prenorm(rows, mix):
        x1 = x_ref[0, rows, :] + _rmsnorm(mix, gpost_ref[...])
        return x1, _rmsnorm(x1, gxpre_ref[...]).astype(BF16)

    def attend(h):
        q = _dot(h, wxq_ref[...])
        heads = []
        for hh in range(X_HEADS):
            cols = slice(hh * X_HEAD_DIM, (hh + 1) * X_HEAD_DIM)
            s = _dot_nt(q[:, cols].astype(BF16), kx_ref[0, :, cols]) * scale
            e = jnp.exp(s - jnp.max(s, axis=-1, keepdims=True))
            p = e / jnp.sum(e, axis=-1, keepdims=True)
            heads.append(_dot(p.astype(BF16), vx_ref[0, :, cols]))
        return _dot(jnp.concatenate(heads, axis=1).astype(BF16), wxo_ref[...])

    rows = slice(0, tm)
    x1, h = residual_prenorm(rows, mix_proj(gate_norm(rows)))
    out_ref[0] = x1 + _rmsnorm(attend(h), gxpost_ref[...])


def _mix_xattn(x, yf, yb, z, o, kx, vx, gssm, gatt, wout, gpost, gxpre, wxq, wxo, gxpost, tm):
    b, l, d = x.shape
    n_mem = kx.shape[1]
    tok = lambda n: pl.BlockSpec((1, tm, n), lambda bi, i: (bi, i, 0))
    mem = pl.BlockSpec((1, n_mem, X_INNER), lambda bi, i: (bi, 0, 0))
    consts = (gssm, gatt, wout, gpost, gxpre, wxq, wxo, gxpost)
    return pl.pallas_call(
        _mix_xattn_kernel,
        grid=(b, l // tm),
        in_specs=[tok(d), tok(SSM_INNER), tok(SSM_INNER), tok(SSM_INNER), tok(ATTN_INNER), mem, mem]
                 + [_const_spec(c.shape) for c in consts],
        out_specs=tok(d),
        out_shape=jax.ShapeDtypeStruct((b, l, d), F32),
        compiler_params=_params("parallel", "parallel"),
        name="mix_xattn",
    )(x, yf, yb, z, o, kx, vx, *consts)


def _ffn_kernel(x_ref, gpre_ref, wg_ref, wu_ref, wd_ref, gpost_ref, out_ref):
    x = x_ref[0]
    h = _rmsnorm(x, gpre_ref[...]).astype(BF16)
    gt = _dot(h, wg_ref[...])
    up = _dot(h, wu_ref[...])
    f = _dot((gt * _sigmoid(gt) * up).astype(BF16), wd_ref[...])
    out_ref[0] = x + _rmsnorm(f, gpost_ref[...])


def _ffn(x, gpre, wg, wu, wd, gpost, tm):
    b, l, d = x.shape
    tok = pl.BlockSpec((1, tm, d), lambda bi, i: (bi, i, 0))
    single = lambda a: pl.BlockSpec(a.shape, lambda *_: (0,) * a.ndim, pipeline_mode=pl.Buffered(1))
    return pl.pallas_call(
        _ffn_kernel,
        grid=(b, l // tm),
        in_specs=[tok, _const_spec(gpre.shape), single(wg), single(wu), single(wd), _const_spec(gpost.shape)],
        out_specs=tok,
        out_shape=jax.ShapeDtypeStruct((b, l, d), F32),
        compiler_params=_params("parallel", "parallel"),
        name="ffn",
    )(x, gpre, wg, wu, wd, gpost)


def _rope_tables(l):
    t = jnp.arange(l, dtype=jnp.int32)
    r = (t // GRID_W).astype(F32)
    c = (t % GRID_W).astype(F32)
    half = ATTN_HEAD_DIM // 2
    inv = 1.0 / (ROPE_THETA ** (jnp.arange(0, half, 2, dtype=F32) / half))
    ang_r = r[:, None] * inv
    ang_c = c[:, None] * inv
    cos = jnp.concatenate([jnp.cos(ang_r)] * 2 + [jnp.cos(ang_c)] * 2, axis=1)
    sin = jnp.concatenate([-jnp.sin(ang_r), jnp.sin(ang_r), -jnp.sin(ang_c), jnp.sin(ang_c)], axis=1)
    return jnp.tile(cos, (1, 2)), jnp.tile(sin, (1, 2))


def _tile(l, want):
    t = min(l, want)
    assert l % t == 0
    return t


def _layer(x, mem, w):
    b, l, _ = x.shape
    assert l % CHUNK == 0 and l % GRID_W == 0
    tm = _tile(l, 512)
    cos, sin = _rope_tables(l)
    z, xc, dt, qp, kp, vt = _in_proj(x, w["g_pre"], w["wxbc"], w["wrest"], cos, sin, w["qg"], w["kg"],
                                     w["conv_w"], w["conv_b"], tm)
    yf, yb = _ssd(xc, dt, w["dt_bias"], w["a_log"], w["d_skip"], _tile(l, 4 * CHUNK) // CHUNK)
    o = _attention(w["attn_bounded"], qp, kp, vt, _tile(l, 1024), _tile(l, 2048))
    kx, vx = _mem_kv(mem, w["g_mem"], w["wxk"], w["wxv"])
    x2 = _mix_xattn(x, yf, yb, z, o, kx, vx, w["g_ssm"], w["g_att"], w["wout"], w["g_post"],
                    w["g_xpre"], w["wxq"], w["wxo"], w["g_xpost"], tm)
    return _ffn(x2, w["g_fpre"], w["wg"], w["wu"], w["wd"], w["g_fpost"], tm)


def _prep_weights(i, norm_mix_pre, w_in, conv_w, conv_b, dt_bias, a_log, d_skip, ssm_norm, q_norm, k_norm,
                  attn_norm, w_out, norm_mix_post, norm_x_pre, norm_mem, w_xq, w_xk, w_xv, w_xo, norm_x_post,
                  norm_ffn_pre, w_gate, w_up, w_down, norm_ffn_post):
    row = lambda v: v.reshape(1, -1).astype(F32)
    splits = np.cumsum([SSM_INNER, CONV_DIM, 2 * SSM_HEADS, ATTN_INNER,
                        ATTN_KV_HEADS * ATTN_HEAD_DIM, ATTN_KV_HEADS * ATTN_HEAD_DIM])[:-1]
    wz, wxbc, wdt, wq, wk, wv = jnp.split(w_in[i], [int(s) for s in splits], axis=-1)
    order = [h + ATTN_GROUP * g for h in range(ATTN_GROUP) for g in range(ATTN_KV_HEADS)]
    wq = wq.reshape(D_MODEL, ATTN_HEADS, ATTN_HEAD_DIM)[:, order, :].reshape(D_MODEL, ATTN_INNER)
    pad = DT_PAD - 2 * SSM_HEADS
    q_gain = row(q_norm[i]) * (ATTN_HEAD_DIM ** -0.5 * np.log2(np.e))
    k_gain = row(k_norm[i])
    score_bound = ATTN_HEAD_DIM * jnp.max(jnp.abs(q_gain)) * jnp.max(jnp.abs(k_gain))
    return {
        "attn_bounded": (score_bound * BOUND_MARGIN <= MAX_EXP2_SCORE).astype(jnp.int32).reshape(1),
        "g_pre": row(norm_mix_pre[i]),
        "wxbc": wxbc.astype(BF16),
        "wrest": jnp.concatenate([wz, wq, wk, wv, jnp.pad(wdt, ((0, 0), (0, pad)))], axis=1).astype(BF16),
        "qg": jnp.tile(q_gain, (1, LANES // ATTN_HEAD_DIM)),
        "kg": jnp.tile(k_gain, (1, LANES // ATTN_HEAD_DIM)),
        "conv_w": conv_w[i].astype(F32), "conv_b": row(conv_b[i]),
        "dt_bias": jnp.pad(row(dt_bias[i]), ((0, 0), (0, pad))),
        "a_log": jnp.pad(row(a_log[i]), ((0, 0), (0, pad))),
        "d_skip": jnp.repeat(row(d_skip[i]), SSM_HEAD_DIM, axis=1),
        "g_ssm": row(ssm_norm[i]), "g_att": row(attn_norm[i]),
        "wout": w_out[i].astype(BF16),
        "g_post": row(norm_mix_post[i]), "g_xpre": row(norm_x_pre[i]), "g_mem": row(norm_mem[i]),
        "wxq": w_xq[i].astype(BF16), "wxk": w_xk[i].astype(BF16), "wxv": w_xv[i].astype(BF16),
        "wxo": w_xo[i].astype(BF16), "g_xpost": row(norm_x_post[i]),
        "g_fpre": row(norm_ffn_pre[i]), "wg": w_gate[i].astype(BF16), "wu": w_up[i].astype(BF16),
        "wd": w_down[i].astype(BF16), "g_fpost": row(norm_ffn_post[i]),
    }


def kernel(x_prompt, x_sample, mem_prompt, mem_sample, norm_mix_pre, w_in, conv_w, conv_b, dt_bias, a_log,
           d_skip, ssm_norm, q_norm, k_norm, attn_norm, w_out, norm_mix_post, norm_x_pre, norm_mem, w_xq,
           w_xk, w_xv, w_xo, norm_x_post, norm_ffn_pre, w_gate, w_up, w_down, norm_ffn_post):
    y_prompt, y_sample = x_prompt, x_sample
    for i in range(w_in.shape[0]):
        w = _prep_weights(i, norm_mix_pre, w_in, conv_w, conv_b, dt_bias, a_log, d_skip, ssm_norm, q_norm,
                          k_norm, attn_norm, w_out, norm_mix_post, norm_x_pre, norm_mem, w_xq, w_xk, w_xv,
                          w_xo, norm_x_post, norm_ffn_pre, w_gate, w_up, w_down, norm_ffn_post)
        y_prompt = _layer(y_prompt, mem_prompt, w)
        y_sample = _layer(y_sample, mem_sample, w)
    return (y_prompt, y_sample)
```

```python
import functools

import numpy as np
import jax
import jax.numpy as jnp
from jax import lax
from jax.experimental import pallas as pl
from jax.experimental.pallas import tpu as pltpu

F32 = jnp.float32
BF16 = jnp.bfloat16

EPS = 1e-6
LANES = 128
SUBLANES = 8
VMEM_LIMIT_BYTES = 56 * 1024 * 1024

D_MODEL = 1024
SSM_HEADS = 16
SSM_HEAD_DIM = 64
SSM_INNER = SSM_HEADS * SSM_HEAD_DIM
SSM_GROUPS = 2
SSM_STATE = 128
D_CONV = 5
CHUNK = 128
CONV_DIM = SSM_INNER + 2 * SSM_GROUPS * SSM_STATE
ATTN_HEADS = 8
ATTN_KV_HEADS = 2
ATTN_HEAD_DIM = 64
ATTN_INNER = ATTN_HEADS * ATTN_HEAD_DIM
ATTN_GROUP = ATTN_HEADS // ATTN_KV_HEADS
GRID_W = 64
ROPE_THETA = 10000.0
X_HEADS = 4
X_HEAD_DIM = 128
X_INNER = X_HEADS * X_HEAD_DIM
DT_PAD = LANES
HALO = SUBLANES
MAX_EXP2_SCORE = 60.0
BOUND_MARGIN = 1.02


def _dot(a, b):
    return jnp.dot(a, b, preferred_element_type=F32)


def _dot_nt(a, b):
    return lax.dot_general(a, b, (((1,), (1,)), ((), ())), preferred_element_type=F32)


def _dot_tn(a, b):
    return lax.dot_general(a, b, (((0,), (0,)), ((), ())), preferred_element_type=F32)


def _rmsnorm(x, g):
    return x * lax.rsqrt(jnp.mean(x * x, axis=-1, keepdims=True) + EPS) * g


LOG2E = float(np.log2(np.e))


def _sigmoid(x):
    return 1.0 / (1.0 + jnp.exp2(x * -LOG2E))


def _params(*sem):
    return pltpu.CompilerParams(dimension_semantics=sem, vmem_limit_bytes=VMEM_LIMIT_BYTES)


def _const_spec(shape):
    nd = len(shape)
    return pl.BlockSpec(shape, lambda *_: (0,) * nd)


def _in_proj_kernel(x_ref, xp_ref, xn_ref, g_ref, wxbc_ref, wrest_ref,
                    cos_ref, sin_ref, qg_ref, kg_ref, cw_ref, cb_ref,
                    z_ref, xc_ref, dt_ref, qp_ref, kp_ref, vt_ref, ext_ref, conv_ref):
    i = pl.program_id(1)
    n = pl.num_programs(1)
    tm = x_ref.shape[1]
    x_all = jnp.concatenate([x_ref[0], xp_ref[0], xn_ref[0]], axis=0)
    h_all = _rmsnorm(x_all, g_ref[...]).astype(BF16)
    h = h_all[:tm]

    xbc = _dot(h_all, wxbc_ref[...])
    for s in range(CONV_DIM // LANES):
        cols = slice(s * LANES, (s + 1) * LANES)
        ext_ref[s, 0:HALO, :] = jnp.where(i > 0, xbc[tm:tm + HALO, cols], 0.0)
        ext_ref[s, HALO:HALO + tm, :] = xbc[:tm, cols]
        ext_ref[s, HALO + tm:HALO + tm + HALO, :] = jnp.where(i < n - 1, xbc[tm + HALO:, cols], 0.0)
        ext_ref[s, HALO + tm + HALO:, :] = jnp.zeros((HALO, LANES), F32)
    pitch = tm // SUBLANES + 1
    for s in range(CONV_DIM // LANES):
        cols = slice(s * LANES, (s + 1) * LANES)
        taps = [cw_ref[j:j + 1, cols] for j in range(D_CONV)]
        bias = cb_ref[:, cols]
        for a in range(pitch):
            acc = bias
            for j in range(D_CONV):
                start = HALO - D_CONV // 2 + j + a
                acc = acc + ext_ref[s, pl.ds(start, SUBLANES, stride=pitch), :] * taps[j]
            conv_ref[s, pl.ds(a, SUBLANES, stride=pitch), :] = acc * _sigmoid(acc)
        xc_ref[0, :, cols] = conv_ref[s, 0:tm, :].astype(BF16)

    rest = _dot(h, wrest_ref[...])
    z_ref[0] = rest[:, :SSM_INNER].astype(BF16)
    q = rest[:, SSM_INNER:SSM_INNER + ATTN_INNER]
    k = rest[:, SSM_INNER + ATTN_INNER:SSM_INNER + ATTN_INNER + LANES]
    v = rest[:, SSM_INNER + ATTN_INNER + LANES:SSM_INNER + ATTN_INNER + 2 * LANES]
    dt_ref[0] = rest[:, SSM_INNER + ATTN_INNER + 2 * LANES:]
    vt_ref[0] = v.T.astype(BF16)

    cos = cos_ref[...]
    sin = sin_ref[...]
    lane = lax.broadcasted_iota(jnp.int32, cos.shape, 1)

    r_id = (lax.broadcasted_iota(jnp.int32, (2 * LANES, LANES), 0) % LANES) // ATTN_HEAD_DIM
    c_id = lax.broadcasted_iota(jnp.int32, (2 * LANES, LANES), 1) // ATTN_HEAD_DIM
    same_head = jnp.where(r_id == c_id, 1.0, 0.0).astype(BF16)

    def head_norm_rope(blk, gain):
        sq = blk * blk
        hi = sq.astype(BF16)
        lo = (sq - hi.astype(F32)).astype(BF16)
        ss = _dot(jnp.concatenate([hi, lo], axis=1), same_head)
        y = blk * lax.rsqrt(ss * (1.0 / ATTN_HEAD_DIM) + EPS) * gain
        partner = jnp.where((lane & 16) != 0, pltpu.roll(y, 16, 1), pltpu.roll(y, LANES - 16, 1))
        return y * cos + partner * sin

    kp_ref[0] = head_norm_rope(k, kg_ref[...]).astype(BF16)
    low = lane < ATTN_HEAD_DIM
    qg = qg_ref[...]
    for j in range(ATTN_GROUP):
        r = head_norm_rope(q[:, j * LANES:(j + 1) * LANES], qg)
        qp_ref[0, :, j * LANES:(j + 1) * LANES] = jnp.where(low, r, 0.0).astype(BF16)
        qp_ref[0, :, (ATTN_GROUP + j) * LANES:(ATTN_GROUP + j + 1) * LANES] = jnp.where(low, 0.0, r).astype(BF16)


def _in_proj(x, g, wxbc, wrest, cos, sin, qg, kg, cw, cb, tm):
    b, l, d = x.shape
    grid = (b, l // tm)
    nh = tm // HALO
    last = l // HALO - 1
    tok = lambda n: pl.BlockSpec((1, tm, n), lambda bi, i: (bi, i, 0))
    pos = pl.BlockSpec((tm, LANES), lambda bi, i: (i, 0))
    prev_rows = pl.BlockSpec((1, HALO, d), lambda bi, i: (bi, jnp.maximum(i * nh - 1, 0), 0))
    next_rows = pl.BlockSpec((1, HALO, d), lambda bi, i: (bi, jnp.minimum((i + 1) * nh, last), 0))
    out_shape = (
        jax.ShapeDtypeStruct((b, l, SSM_INNER), BF16),
        jax.ShapeDtypeStruct((b, l, CONV_DIM), BF16),
        jax.ShapeDtypeStruct((b, l, DT_PAD), F32),
        jax.ShapeDtypeStruct((b, l, ATTN_HEADS * LANES), BF16),
        jax.ShapeDtypeStruct((b, l, LANES), BF16),
        jax.ShapeDtypeStruct((b, LANES, l), BF16),
    )
    return pl.pallas_call(
        _in_proj_kernel,
        grid=grid,
        in_specs=[tok(d), prev_rows, next_rows, _const_spec(g.shape), _const_spec(wxbc.shape),
                  _const_spec(wrest.shape), pos, pos, _const_spec(qg.shape), _const_spec(kg.shape),
                  _const_spec(cw.shape), _const_spec(cb.shape)],
        out_specs=(tok(SSM_INNER), tok(CONV_DIM), tok(DT_PAD), tok(ATTN_HEADS * LANES), tok(LANES),
                   pl.BlockSpec((1, LANES, tm), lambda bi, i: (bi, 0, i))),
        out_shape=out_shape,
        scratch_shapes=[pltpu.VMEM((CONV_DIM // LANES, tm + 3 * HALO, LANES), F32),
                        pltpu.VMEM((CONV_DIM // LANES, tm + SUBLANES, LANES), F32)],
        compiler_params=_params("parallel", "parallel"),
        name="in_proj",
    )(x, x, x, g, wxbc, wrest, cos, sin, qg, kg, cw, cb)


def _expand_heads(w, sel):
    hi = w.astype(BF16)
    lo = (w - hi.astype(F32)).astype(BF16)
    return _dot(jnp.concatenate([hi, lo], axis=1), sel)


def _chunk_mask(reverse):
    row = lax.broadcasted_iota(jnp.int32, (CHUNK, CHUNK), 0)
    col = lax.broadcasted_iota(jnp.int32, (CHUNK, CHUNK), 1)
    return (row <= col) if reverse else (row >= col)


def _ssd_decays(dt_raw, dtb, a_neg, sel, reverse):
    n_chunks = dt_raw.shape[0] // CHUNK
    tri = _chunk_mask(reverse).astype(F32)
    end = 0 if reverse else CHUNK - 1
    xpre = dt_raw + dtb
    dtv = jnp.maximum(xpre, 0.0) + jnp.log1p(jnp.exp(-jnp.abs(xpre)))
    a = dtv * a_neg
    cs_chunks = [jnp.dot(tri, a[c * CHUNK:(c + 1) * CHUNK], preferred_element_type=F32,
                         precision=lax.Precision.HIGHEST) for c in range(n_chunks)]
    cs = jnp.concatenate(cs_chunks, axis=0)
    cs_end = jnp.concatenate([jnp.broadcast_to(c[end:end + 1, :], (CHUNK, LANES)) for c in cs_chunks], axis=0)
    e3 = _expand_heads(jnp.exp(cs), sel)
    e2 = _expand_heads(dtv * jnp.exp(cs_end - cs), sel)
    src_t = [(c - jnp.log(dtv[i * CHUNK:(i + 1) * CHUNK])).T for i, c in enumerate(cs_chunks)]
    return cs_chunks, src_t, e2, e3


def _ssd_chunk(xc, cs, src_t, e2, e3, h_ref, reverse):
    off = SSM_HEADS if reverse else 0
    mask = _chunk_mask(reverse)
    end = 0 if reverse else CHUNK - 1
    xs = xc[:, :SSM_INNER]
    xdec = (xs.astype(F32) * e2).astype(BF16)
    lane = lax.broadcasted_iota(jnp.int32, (CHUNK, LANES), 1)
    low = lane < SSM_HEAD_DIM
    zero = jnp.zeros((CHUNK, LANES), BF16)
    gw = SSM_INNER // SSM_GROUPS
    hpg = SSM_HEADS // SSM_GROUPS
    y_tiles = []
    for g in range(SSM_GROUPS):
        bm = xc[:, SSM_INNER + g * SSM_STATE:SSM_INNER + (g + 1) * SSM_STATE]
        cm = xc[:, SSM_INNER + (SSM_GROUPS + g) * SSM_STATE:SSM_INNER + (SSM_GROUPS + g + 1) * SSM_STATE]
        cb = _dot_nt(cm, bm)
        h_prev = h_ref[:, g * gw:(g + 1) * gw]
        y_off = _dot(cm, h_prev.astype(BF16)) * e3[:, g * gw:(g + 1) * gw]
        for jj in range(hpg // 2):
            j = g * (hpg // 2) + jj
            x_tile = xs[:, j * LANES:(j + 1) * LANES]
            ms = []
            for half in range(2):
                hd = off + 2 * j + half
                seg = cs[:, hd:hd + 1] - src_t[hd:hd + 1, :]
                ms.append((cb * jnp.exp(jnp.where(mask, seg, -jnp.inf))).astype(BF16))
            x_pair = jnp.concatenate([jnp.where(low, x_tile, zero), jnp.where(low, zero, x_tile)], axis=0)
            y_tiles.append(y_off[:, jj * LANES:(jj + 1) * LANES] + _dot(jnp.concatenate(ms, axis=1), x_pair))
        s_new = _dot_tn(bm, xdec[:, g * gw:(g + 1) * gw])
        h_ref[:, g * gw:(g + 1) * gw] = h_prev * e3[end:end + 1, g * gw:(g + 1) * gw] + s_new
    return jnp.concatenate(y_tiles, axis=1)


def _ssd_kernel(xf_ref, xb_ref, dtf_ref, dtb_ref, bias_ref, alog_ref, dskip_ref, sel_ref,
                yf_ref, yb_ref, hf_ref, hb_ref):
    c = pl.program_id(1)

    @pl.when(c == 0)
    def _():
        hf_ref[...] = jnp.zeros_like(hf_ref)
        hb_ref[...] = jnp.zeros_like(hb_ref)

    a_neg = -jnp.exp(alog_ref[...])
    bias = bias_ref[...]
    n_chunks = xf_ref.shape[1] // CHUNK
    dec_f = _ssd_decays(dtf_ref[0], bias, a_neg, sel_ref[0], reverse=False)
    dec_b = _ssd_decays(dtb_ref[0], bias, a_neg, sel_ref[1], reverse=True)

    def chunk(x_ref, dec, h_ref, c, reverse):
        cs, src_t, e2, e3 = dec
        rows = slice(c * CHUNK, (c + 1) * CHUNK)
        xc = x_ref[0, rows, :]
        return xc, _ssd_chunk(xc, cs[c], src_t[c], e2[rows], e3[rows], h_ref, reverse)

    for c in range(n_chunks):
        xf, yf = chunk(xf_ref, dec_f, hf_ref, c, False)
        yf_ref[0, c * CHUNK:(c + 1) * CHUNK, :] = (
            yf + xf[:, :SSM_INNER].astype(F32) * dskip_ref[...]).astype(BF16)
        cb = n_chunks - 1 - c
        _, yb = chunk(xb_ref, dec_b, hb_ref, cb, True)
        yb_ref[0, cb * CHUNK:(cb + 1) * CHUNK, :] = yb.astype(BF16)


def _ssd(xc, dt, bias, alog, dskip, n_chunks):
    b, l, _ = xc.shape
    rows = n_chunks * CHUNK
    nc = l // rows
    fwd = lambda n: pl.BlockSpec((1, rows, n), lambda bi, c: (bi, c, 0))
    bwd = lambda n: pl.BlockSpec((1, rows, n), lambda bi, c: (bi, nc - 1 - c, 0))
    lane_head = np.arange(SSM_INNER) // SSM_HEAD_DIM
    sel = np.stack([np.arange(DT_PAD)[:, None] == (d * SSM_HEADS + lane_head)[None, :] for d in range(2)])
    sel = jnp.asarray(np.concatenate([sel, sel], axis=1), BF16)
    return pl.pallas_call(
        _ssd_kernel,
        grid=(b, nc),
        in_specs=[fwd(CONV_DIM), bwd(CONV_DIM), fwd(DT_PAD), bwd(DT_PAD),
                  _const_spec(bias.shape), _const_spec(alog.shape), _const_spec(dskip.shape),
                  _const_spec(sel.shape)],
        out_specs=(fwd(SSM_INNER), bwd(SSM_INNER)),
        out_shape=(jax.ShapeDtypeStruct((b, l, SSM_INNER), BF16),
                   jax.ShapeDtypeStruct((b, l, SSM_INNER), BF16)),
        scratch_shapes=[pltpu.VMEM((SSM_STATE, SSM_INNER), F32),
                        pltpu.VMEM((SSM_STATE, SSM_INNER), F32)],
        compiler_params=_params("parallel", "arbitrary"),
        name="ssd",
    )(xc, xc, dt, dt, bias, alog, dskip, sel)


def _attn_kernel(bounded_ref, q_ref, k_ref, vt_ref, o_ref, m_ref, l_ref, acc_ref, *, tk):
    nk = k_ref.shape[1] // tk
    l_ref[...] = jnp.zeros_like(l_ref)
    acc_ref[...] = jnp.zeros_like(acc_ref)

    def tiles(i):
        ks = pl.multiple_of(i * tk, tk)
        return k_ref[0, pl.ds(ks, tk), :], vt_ref[0, :, pl.ds(ks, tk)]

    def bounded_body(i, carry):
        k, vt = tiles(i)
        for j in range(ATTN_GROUP):
            rows = slice(j * ATTN_HEAD_DIM, (j + 1) * ATTN_HEAD_DIM)
            p_t = jnp.exp2(_dot_nt(k, q_ref[0, :, j * LANES:(j + 1) * LANES]))
            part = p_t[0:SUBLANES, :]
            for r in range(1, tk // SUBLANES):
                part = part + p_t[r * SUBLANES:(r + 1) * SUBLANES, :]
            l_ref[j * SUBLANES:(j + 1) * SUBLANES, :] += part
            acc_ref[rows, :] += _dot(vt, p_t.astype(BF16))
        return carry

    def online_body(i, carry):
        k, vt = tiles(i)
        for j in range(ATTN_GROUP):
            rows = slice(j * ATTN_HEAD_DIM, (j + 1) * ATTN_HEAD_DIM)
            s_t = _dot_nt(k, q_ref[0, :, j * LANES:(j + 1) * LANES])
            m_prev = m_ref[j:j + 1, :]
            m_new = jnp.maximum(m_prev, jnp.max(s_t, axis=0, keepdims=True))
            alpha = jnp.exp2(m_prev - m_new)
            p_t = jnp.exp2(s_t - m_new)
            lrow = slice(j * SUBLANES, j * SUBLANES + 1)
            l_ref[lrow, :] = alpha * l_ref[lrow, :] + jnp.sum(p_t, axis=0, keepdims=True)
            acc_ref[rows, :] = alpha * acc_ref[rows, :] + _dot(vt, p_t.astype(BF16))
            m_ref[j:j + 1, :] = m_new
        return carry

    bounded = bounded_ref[0] != 0

    @pl.when(bounded)
    def _():
        lax.fori_loop(0, nk, bounded_body, 0)

    @pl.when(jnp.logical_not(bounded))
    def _():
        m_ref[...] = jnp.full_like(m_ref, -jnp.inf)
        lax.fori_loop(0, nk, online_body, 0)

    for j in range(ATTN_GROUP):
        rows = slice(j * ATTN_HEAD_DIM, (j + 1) * ATTN_HEAD_DIM)
        l = jnp.sum(l_ref[j * SUBLANES:(j + 1) * SUBLANES, :], axis=0, keepdims=True)
        acc_ref[rows, :] = acc_ref[rows, :] / l
    o_ref[0] = acc_ref[...].T


def _attention(bounded, qp, kp, vt, tq, tk):
    b, l, _ = qp.shape
    gw = ATTN_GROUP * LANES
    return pl.pallas_call(
        functools.partial(_attn_kernel, tk=tk),
        grid=(b, ATTN_KV_HEADS, l // tq),
        in_specs=[pl.BlockSpec(memory_space=pltpu.SMEM),
                  pl.BlockSpec((1, tq, gw), lambda bi, g, i: (bi, i, g)),
                  pl.BlockSpec((1, l, LANES), lambda bi, g, i: (bi, 0, 0)),
                  pl.BlockSpec((1, ATTN_HEAD_DIM, l), lambda bi, g, i: (bi, g, 0))],
        out_specs=pl.BlockSpec((1, tq, ATTN_GROUP * ATTN_HEAD_DIM), lambda bi, g, i: (bi, i, g)),
        out_shape=jax.ShapeDtypeStruct((b, l, ATTN_INNER), F32),
        scratch_shapes=[pltpu.VMEM((SUBLANES, tq), F32), pltpu.VMEM((ATTN_GROUP * SUBLANES, tq), F32),
                        pltpu.VMEM((ATTN_GROUP * ATTN_HEAD_DIM, tq), F32)],
        compiler_params=_params("parallel", "parallel", "parallel"),
        name="attention",
    )(bounded, qp, kp, vt)


def _mem_kv_kernel(mem_ref, g_ref, wk_ref, wv_ref, k_ref, v_ref):
    m = _rmsnorm(mem_ref[0], g_ref[...]).astype(BF16)
    k_ref[0] = _dot(m, wk_ref[...]).astype(BF16)
    v_ref[0] = _dot(m, wv_ref[...]).astype(BF16)


def _mem_kv(mem, g, wk, wv):
    b, n, d = mem.shape
    spec = pl.BlockSpec((1, n, X_INNER), lambda bi: (bi, 0, 0))
    return pl.pallas_call(
        _mem_kv_kernel,
        grid=(b,),
        in_specs=[pl.BlockSpec((1, n, d), lambda bi: (bi, 0, 0)), _const_spec(g.shape),
                  _const_spec(wk.shape), _const_spec(wv.shape)],
        out_specs=(spec, spec),
        out_shape=(jax.ShapeDtypeStruct((b, n, X_INNER), BF16),) * 2,
        compiler_params=_params("parallel"),
        name="mem_kv",
    )(mem, g, wk, wv)


def _mix_xattn_kernel(x_ref, yf_ref, yb_ref, z_ref, o_ref, kx_ref, vx_ref,
                      gssm_ref, gatt_ref, wout_ref, gpost_ref,
                      gxpre_ref, wxq_ref, wxo_ref, gxpost_ref, out_ref):
    tm = x_ref.shape[1]
    gw = SSM_INNER // SSM_GROUPS
    scale = X_HEAD_DIM ** -0.5

    def gate_norm(rows):
        z = z_ref[0, rows, :].astype(F32)
        y = (yf_ref[0, rows, :].astype(F32) + yb_ref[0, rows, :].astype(F32)) * (z * _sigmoid(z))
        yn = jnp.concatenate(
            [y[:, g * gw:(g + 1) * gw]
             * lax.rsqrt(jnp.mean(y[:, g * gw:(g + 1) * gw] * y[:, g * gw:(g + 1) * gw], axis=-1, keepdims=True)
                         + EPS)
             for g in range(SSM_GROUPS)], axis=1)
        return (yn * gssm_ref[...]).astype(BF16), _rmsnorm(o_ref[0, rows, :], gatt_ref[...]).astype(BF16)

    def mix_proj(ys):
        return _dot(ys[0], wout_ref[:SSM_INNER, :]) + _dot(ys[1], wout_ref[SSM_INNER:, :])

    def residual_prenorm(rows, mix):
        x1 = x_ref[0, rows, :] + _rmsnorm(mix, gpost_ref[...])
        return x1, _rmsnorm(x1, gxpre_ref[...]).astype(BF16)

    def attend(h):
        q = _dot(h, wxq_ref[...])
        heads = []
        for hh in range(X_HEADS):
            cols = slice(hh * X_HEAD_DIM, (hh + 1) * X_HEAD_DIM)
            s = _dot_nt(q[:, cols].astype(BF16), kx_ref[0, :, cols])
            e = jnp.exp2((s - jnp.max(s, axis=-1, keepdims=True)) * (scale * LOG2E))
            inv = 1.0 / jnp.sum(e, axis=-1, keepdims=True)
            heads.append(_dot(e.astype(BF16), vx_ref[0, :, cols]) * inv)
        return _dot(jnp.concatenate(heads, axis=1).astype(BF16), wxo_ref[...])

    rows = slice(0, tm)
    x1, h = residual_prenorm(rows, mix_proj(gate_norm(rows)))
    out_ref[0] = x1 + _rmsnorm(attend(h), gxpost_ref[...])


def _mix_xattn(x, yf, yb, z, o, kx, vx, gssm, gatt, wout, gpost, gxpre, wxq, wxo, gxpost, tm):
    b, l, d = x.shape
    n_mem = kx.shape[1]
    tok = lambda n: pl.BlockSpec((1, tm, n), lambda bi, i: (bi, i, 0))
    mem = pl.BlockSpec((1, n_mem, X_INNER), lambda bi, i: (bi, 0, 0))
    consts = (gssm, gatt, wout, gpost, gxpre, wxq, wxo, gxpost)
    return pl.pallas_call(
        _mix_xattn_kernel,
        grid=(b, l // tm),
        in_specs=[tok(d), tok(SSM_INNER), tok(SSM_INNER), tok(SSM_INNER), tok(ATTN_INNER), mem, mem]
                 + [_const_spec(c.shape) for c in consts],
        out_specs=tok(d),
        out_shape=jax.ShapeDtypeStruct((b, l, d), F32),
        compiler_params=_params("parallel", "parallel"),
        name="mix_xattn",
    )(x, yf, yb, z, o, kx, vx, *consts)


def _ffn_kernel(x_ref, gpre_ref, wg_ref, wu_ref, wd_ref, gpost_ref, out_ref):
    x = x_ref[0]
    h = _rmsnorm(x, gpre_ref[...]).astype(BF16)
    gt = _dot(h, wg_ref[...])
    up = _dot(h, wu_ref[...])
    f = _dot((gt * _sigmoid(gt) * up).astype(BF16), wd_ref[...])
    out_ref[0] = x + _rmsnorm(f, gpost_ref[...])


def _ffn(x, gpre, wg, wu, wd, gpost, tm):
    b, l, d = x.shape
    tok = pl.BlockSpec((1, tm, d), lambda bi, i: (bi, i, 0))
    single = lambda a: pl.BlockSpec(a.shape, lambda *_: (0,) * a.ndim, pipeline_mode=pl.Buffered(1))
    return pl.pallas_call(
        _ffn_kernel,
        grid=(b, l // tm),
        in_specs=[tok, _const_spec(gpre.shape), single(wg), single(wu), single(wd), _const_spec(gpost.shape)],
        out_specs=tok,
        out_shape=jax.ShapeDtypeStruct((b, l, d), F32),
        compiler_params=_params("parallel", "parallel"),
        name="ffn",
    )(x, gpre, wg, wu, wd, gpost)


def _rope_tables(l):
    t = jnp.arange(l, dtype=jnp.int32)
    r = (t // GRID_W).astype(F32)
    c = (t % GRID_W).astype(F32)
    half = ATTN_HEAD_DIM // 2
    inv = 1.0 / (ROPE_THETA ** (jnp.arange(0, half, 2, dtype=F32) / half))
    ang_r = r[:, None] * inv
    ang_c = c[:, None] * inv
    cos = jnp.concatenate([jnp.cos(ang_r)] * 2 + [jnp.cos(ang_c)] * 2, axis=1)
    sin = jnp.concatenate([-jnp.sin(ang_r), jnp.sin(ang_r), -jnp.sin(ang_c), jnp.sin(ang_c)], axis=1)
    return jnp.tile(cos, (1, 2)), jnp.tile(sin, (1, 2))


def _tile(l, want):
    t = min(l, want)
    assert l % t == 0
    return t


def _layer(x, mem, w):
    b, l, _ = x.shape
    assert l % CHUNK == 0 and l % GRID_W == 0
    tm = _tile(l, 512)
    cos, sin = _rope_tables(l)
    z, xc, dt, qp, kp, vt = _in_proj(x, w["g_pre"], w["wxbc"], w["wrest"], cos, sin, w["qg"], w["kg"],
                                     w["conv_w"], w["conv_b"], tm)
    yf, yb = _ssd(xc, dt, w["dt_bias"], w["a_log"], w["d_skip"], _tile(l, 8 * CHUNK) // CHUNK)
    o = _attention(w["attn_bounded"], qp, kp, vt, _tile(l, 1024), _tile(l, 2048))
    kx, vx = _mem_kv(mem, w["g_mem"], w["wxk"], w["wxv"])
    x2 = _mix_xattn(x, yf, yb, z, o, kx, vx, w["g_ssm"], w["g_att"], w["wout"], w["g_post"],
                    w["g_xpre"], w["wxq"], w["wxo"], w["g_xpost"], _tile(l, 1024))
    return _ffn(x2, w["g_fpre"], w["wg"], w["wu"], w["wd"], w["g_fpost"], tm)


def _prep_weights(i, norm_mix_pre, w_in, conv_w, conv_b, dt_bias, a_log, d_skip, ssm_norm, q_norm, k_norm,
                  attn_norm, w_out, norm_mix_post, norm_x_pre, norm_mem, w_xq, w_xk, w_xv, w_xo, norm_x_post,
                  norm_ffn_pre, w_gate, w_up, w_down, norm_ffn_post):
    row = lambda v: v.reshape(1, -1).astype(F32)
    splits = np.cumsum([SSM_INNER, CONV_DIM, 2 * SSM_HEADS, ATTN_INNER,
                        ATTN_KV_HEADS * ATTN_HEAD_DIM, ATTN_KV_HEADS * ATTN_HEAD_DIM])[:-1]
    wz, wxbc, wdt, wq, wk, wv = jnp.split(w_in[i], [int(s) for s in splits], axis=-1)
    order = [h + ATTN_GROUP * g for h in range(ATTN_GROUP) for g in range(ATTN_KV_HEADS)]
    wq = wq.reshape(D_MODEL, ATTN_HEADS, ATTN_HEAD_DIM)[:, order, :].reshape(D_MODEL, ATTN_INNER)
    pad = DT_PAD - 2 * SSM_HEADS
    q_gain = row(q_norm[i]) * (ATTN_HEAD_DIM ** -0.5 * np.log2(np.e))
    k_gain = row(k_norm[i])
    score_bound = ATTN_HEAD_DIM * jnp.max(jnp.abs(q_gain)) * jnp.max(jnp.abs(k_gain))
    return {
        "attn_bounded": (score_bound * BOUND_MARGIN <= MAX_EXP2_SCORE).astype(jnp.int32).reshape(1),
        "g_pre": row(norm_mix_pre[i]),
        "wxbc": wxbc.astype(BF16),
        "wrest": jnp.concatenate([wz, wq, wk, wv, jnp.pad(wdt, ((0, 0), (0, pad)))], axis=1).astype(BF16),
        "qg": jnp.tile(q_gain, (1, LANES // ATTN_HEAD_DIM)),
        "kg": jnp.tile(k_gain, (1, LANES // ATTN_HEAD_DIM)),
        "conv_w": conv_w[i].astype(F32), "conv_b": row(conv_b[i]),
        "dt_bias": jnp.pad(row(dt_bias[i]), ((0, 0), (0, pad))),
        "a_log": jnp.pad(row(a_log[i]), ((0, 0), (0, pad))),
        "d_skip": jnp.repeat(row(d_skip[i]), SSM_HEAD_DIM, axis=1),
        "g_ssm": row(ssm_norm[i]), "g_att": row(attn_norm[i]),
        "wout": w_out[i].astype(BF16),
        "g_post": row(norm_mix_post[i]), "g_xpre": row(norm_x_pre[i]), "g_mem": row(norm_mem[i]),
        "wxq": w_xq[i].astype(BF16), "wxk": w_xk[i].astype(BF16), "wxv": w_xv[i].astype(BF16),
        "wxo": w_xo[i].astype(BF16), "g_xpost": row(norm_x_post[i]),
        "g_fpre": row(norm_ffn_pre[i]), "wg": w_gate[i].astype(BF16), "wu": w_up[i].astype(BF16),
        "wd": w_down[i].astype(BF16), "g_fpost": row(norm_ffn_post[i]),
    }


def kernel(x_prompt, x_sample, mem_prompt, mem_sample, norm_mix_pre, w_in, conv_w, conv_b, dt_bias, a_log,
           d_skip, ssm_norm, q_norm, k_norm, attn_norm, w_out, norm_mix_post, norm_x_pre, norm_mem, w_xq,
           w_xk, w_xv, w_xo, norm_x_post, norm_ffn_pre, w_gate, w_up, w_down, norm_ffn_post):
    y_prompt, y_sample = x_prompt, x_sample
    for i in range(w_in.shape[0]):
        w = _prep_weights(i, norm_mix_pre, w_in, conv_w, conv_b, dt_bias, a_log, d_skip, ssm_norm, q_norm,
                          k_norm, attn_norm, w_out, norm_mix_post, norm_x_pre, norm_mem, w_xq, w_xk, w_xv,
                          w_xo, norm_x_post, norm_ffn_pre, w_gate, w_up, w_down, norm_ffn_post)
        y_prompt = _layer(y_prompt, mem_prompt, w)
        y_sample = _layer(y_sample, mem_sample, w)
    return (y_prompt, y_sample)
```

```python
import functools

import numpy as np
import jax
import jax.numpy as jnp
from jax import lax
from jax.experimental import pallas as pl
from jax.experimental.pallas import tpu as pltpu

F32 = jnp.float32
BF16 = jnp.bfloat16

EPS = 1e-6
LANES = 128
SUBLANES = 8
VMEM_LIMIT_BYTES = 56 * 1024 * 1024

D_MODEL = 1024
SSM_HEADS = 16
SSM_HEAD_DIM = 64
SSM_INNER = SSM_HEADS * SSM_HEAD_DIM
SSM_GROUPS = 2
SSM_STATE = 128
D_CONV = 5
CHUNK = 128
CONV_DIM = SSM_INNER + 2 * SSM_GROUPS * SSM_STATE
ATTN_HEADS = 8
ATTN_KV_HEADS = 2
ATTN_HEAD_DIM = 64
ATTN_INNER = ATTN_HEADS * ATTN_HEAD_DIM
ATTN_GROUP = ATTN_HEADS // ATTN_KV_HEADS
GRID_W = 64
ROPE_THETA = 10000.0
X_HEADS = 4
X_HEAD_DIM = 128
X_INNER = X_HEADS * X_HEAD_DIM
DT_PAD = LANES
HALO = SUBLANES
MAX_EXP2_SCORE = 60.0
BOUND_MARGIN = 1.02


def _dot(a, b):
    return jnp.dot(a, b, preferred_element_type=F32)


def _dot_nt(a, b):
    return lax.dot_general(a, b, (((1,), (1,)), ((), ())), preferred_element_type=F32)


def _dot_tn(a, b):
    return lax.dot_general(a, b, (((0,), (0,)), ((), ())), preferred_element_type=F32)


def _rmsnorm(x, g):
    return x * lax.rsqrt(jnp.mean(x * x, axis=-1, keepdims=True) + EPS) * g


LOG2E = float(np.log2(np.e))


def _sigmoid(x):
    return 1.0 / (1.0 + jnp.exp2(x * -LOG2E))


def _params(*sem):
    return pltpu.CompilerParams(dimension_semantics=sem, vmem_limit_bytes=VMEM_LIMIT_BYTES)


def _const_spec(shape):
    nd = len(shape)
    return pl.BlockSpec(shape, lambda *_: (0,) * nd)


def _in_proj_kernel(x_ref, xp_ref, xn_ref, g_ref, wxbc_ref, wrest_ref,
                    cos_ref, sin_ref, qg_ref, kg_ref, cw_ref, cb_ref,
                    z_ref, xc_ref, dt_ref, qp_ref, kp_ref, vt_ref, ext_ref, conv_ref):
    i = pl.program_id(1)
    n = pl.num_programs(1)
    tm = x_ref.shape[1]
    x_all = jnp.concatenate([x_ref[0], xp_ref[0], xn_ref[0]], axis=0)
    h_all = _rmsnorm(x_all, g_ref[...]).astype(BF16)
    h = h_all[:tm]

    xbc = _dot(h_all, wxbc_ref[...])
    for s in range(CONV_DIM // LANES):
        cols = slice(s * LANES, (s + 1) * LANES)
        ext_ref[s, 0:HALO, :] = jnp.where(i > 0, xbc[tm:tm + HALO, cols], 0.0)
        ext_ref[s, HALO:HALO + tm, :] = xbc[:tm, cols]
        ext_ref[s, HALO + tm:HALO + tm + HALO, :] = jnp.where(i < n - 1, xbc[tm + HALO:, cols], 0.0)
        ext_ref[s, HALO + tm + HALO:, :] = jnp.zeros((HALO, LANES), F32)
    pitch = tm // SUBLANES + 1
    for s in range(CONV_DIM // LANES):
        cols = slice(s * LANES, (s + 1) * LANES)
        taps = [cw_ref[j:j + 1, cols] for j in range(D_CONV)]
        bias = cb_ref[:, cols]
        for a in range(pitch):
            acc = bias
            for j in range(D_CONV):
                start = HALO - D_CONV // 2 + j + a
                acc = acc + ext_ref[s, pl.ds(start, SUBLANES, stride=pitch), :] * taps[j]
            conv_ref[s, pl.ds(a, SUBLANES, stride=pitch), :] = acc * _sigmoid(acc)
        xc_ref[0, :, cols] = conv_ref[s, 0:tm, :].astype(BF16)

    rest = _dot(h, wrest_ref[...])
    z_ref[0] = rest[:, :SSM_INNER].astype(BF16)
    q = rest[:, SSM_INNER:SSM_INNER + ATTN_INNER]
    k = rest[:, SSM_INNER + ATTN_INNER:SSM_INNER + ATTN_INNER + LANES]
    v = rest[:, SSM_INNER + ATTN_INNER + LANES:SSM_INNER + ATTN_INNER + 2 * LANES]
    dt_ref[0] = rest[:, SSM_INNER + ATTN_INNER + 2 * LANES:]
    vt_ref[0] = v.T.astype(BF16)

    cos = cos_ref[...]
    sin = sin_ref[...]
    lane = lax.broadcasted_iota(jnp.int32, cos.shape, 1)

    r_id = (lax.broadcasted_iota(jnp.int32, (2 * LANES, LANES), 0) % LANES) // ATTN_HEAD_DIM
    c_id = lax.broadcasted_iota(jnp.int32, (2 * LANES, LANES), 1) // ATTN_HEAD_DIM
    same_head = jnp.where(r_id == c_id, 1.0, 0.0).astype(BF16)

    def head_norm_rope(blk, gain):
        sq = blk * blk
        hi = sq.astype(BF16)
        lo = (sq - hi.astype(F32)).astype(BF16)
        ss = _dot(jnp.concatenate([hi, lo], axis=1), same_head)
        y = blk * lax.rsqrt(ss * (1.0 / ATTN_HEAD_DIM) + EPS) * gain
        partner = jnp.where((lane & 16) != 0, pltpu.roll(y, 16, 1), pltpu.roll(y, LANES - 16, 1))
        return y * cos + partner * sin

    kp_ref[0] = head_norm_rope(k, kg_ref[...]).astype(BF16)
    low = lane < ATTN_HEAD_DIM
    qg = qg_ref[...]
    for j in range(ATTN_GROUP):
        r = head_norm_rope(q[:, j * LANES:(j + 1) * LANES], qg)
        qp_ref[0, :, j * LANES:(j + 1) * LANES] = jnp.where(low, r, 0.0).astype(BF16)
        qp_ref[0, :, (ATTN_GROUP + j) * LANES:(ATTN_GROUP + j + 1) * LANES] = jnp.where(low, 0.0, r).astype(BF16)


def _in_proj(x, g, wxbc, wrest, cos, sin, qg, kg, cw, cb, tm):
    b, l, d = x.shape
    grid = (b, l // tm)
    nh = tm // HALO
    last = l // HALO - 1
    tok = lambda n: pl.BlockSpec((1, tm, n), lambda bi, i: (bi, i, 0))
    pos = pl.BlockSpec((tm, LANES), lambda bi, i: (i, 0))
    prev_rows = pl.BlockSpec((1, HALO, d), lambda bi, i: (bi, jnp.maximum(i * nh - 1, 0), 0))
    next_rows = pl.BlockSpec((1, HALO, d), lambda bi, i: (bi, jnp.minimum((i + 1) * nh, last), 0))
    out_shape = (
        jax.ShapeDtypeStruct((b, l, SSM_INNER), BF16),
        jax.ShapeDtypeStruct((b, l, CONV_DIM), BF16),
        jax.ShapeDtypeStruct((b, l, DT_PAD), F32),
        jax.ShapeDtypeStruct((b, l, ATTN_HEADS * LANES), BF16),
        jax.ShapeDtypeStruct((b, l, LANES), BF16),
        jax.ShapeDtypeStruct((b, LANES, l), BF16),
    )
    return pl.pallas_call(
        _in_proj_kernel,
        grid=grid,
        in_specs=[tok(d), prev_rows, next_rows, _const_spec(g.shape), _const_spec(wxbc.shape),
                  _const_spec(wrest.shape), pos, pos, _const_spec(qg.shape), _const_spec(kg.shape),
                  _const_spec(cw.shape), _const_spec(cb.shape)],
        out_specs=(tok(SSM_INNER), tok(CONV_DIM), tok(DT_PAD), tok(ATTN_HEADS * LANES), tok(LANES),
                   pl.BlockSpec((1, LANES, tm), lambda bi, i: (bi, 0, i))),
        out_shape=out_shape,
        scratch_shapes=[pltpu.VMEM((CONV_DIM // LANES, tm + 3 * HALO, LANES), F32),
                        pltpu.VMEM((CONV_DIM // LANES, tm + SUBLANES, LANES), F32)],
        compiler_params=_params("parallel", "parallel"),
        name="in_proj",
    )(x, x, x, g, wxbc, wrest, cos, sin, qg, kg, cw, cb)


def _expand_heads(w, sel):
    hi = w.astype(BF16)
    lo = (w - hi.astype(F32)).astype(BF16)
    return _dot(jnp.concatenate([hi, lo], axis=1), sel)


def _chunk_mask(reverse):
    row = lax.broadcasted_iota(jnp.int32, (CHUNK, CHUNK), 0)
    col = lax.broadcasted_iota(jnp.int32, (CHUNK, CHUNK), 1)
    return (row <= col) if reverse else (row >= col)


def _ssd_decays(dt_raw, dtb, a_neg, sel, reverse):
    n_chunks = dt_raw.shape[0] // CHUNK
    tri = _chunk_mask(reverse).astype(F32)
    end = 0 if reverse else CHUNK - 1
    xpre = dt_raw + dtb
    dtv = jnp.maximum(xpre, 0.0) + jnp.log1p(jnp.exp(-jnp.abs(xpre)))
    a = dtv * a_neg
    cs_chunks = [jnp.dot(tri, a[c * CHUNK:(c + 1) * CHUNK], preferred_element_type=F32,
                         precision=lax.Precision.HIGHEST) for c in range(n_chunks)]
    cs = jnp.concatenate(cs_chunks, axis=0)
    cs_end = jnp.concatenate([jnp.broadcast_to(c[end:end + 1, :], (CHUNK, LANES)) for c in cs_chunks], axis=0)
    e3 = _expand_heads(jnp.exp(cs), sel)
    e2 = _expand_heads(dtv * jnp.exp(cs_end - cs), sel)
    src_t = [(c - jnp.log(dtv[i * CHUNK:(i + 1) * CHUNK])).T for i, c in enumerate(cs_chunks)]
    return cs_chunks, src_t, e2, e3


def _ssd_chunk(xc, cs, src_t, e2, e3, h_ref, reverse):
    off = SSM_HEADS if reverse else 0
    mask = _chunk_mask(reverse)
    end = 0 if reverse else CHUNK - 1
    xs = xc[:, :SSM_INNER]
    xdec = (xs.astype(F32) * e2).astype(BF16)
    lane = lax.broadcasted_iota(jnp.int32, (CHUNK, LANES), 1)
    low = lane < SSM_HEAD_DIM
    zero = jnp.zeros((CHUNK, LANES), BF16)
    gw = SSM_INNER // SSM_GROUPS
    hpg = SSM_HEADS // SSM_GROUPS
    y_tiles = []
    for g in range(SSM_GROUPS):
        bm = xc[:, SSM_INNER + g * SSM_STATE:SSM_INNER + (g + 1) * SSM_STATE]
        cm = xc[:, SSM_INNER + (SSM_GROUPS + g) * SSM_STATE:SSM_INNER + (SSM_GROUPS + g + 1) * SSM_STATE]
        cb = _dot_nt(cm, bm)
        h_prev = h_ref[:, g * gw:(g + 1) * gw]
        y_off = _dot(cm, h_prev.astype(BF16)) * e3[:, g * gw:(g + 1) * gw]
        for jj in range(hpg // 2):
            j = g * (hpg // 2) + jj
            x_tile = xs[:, j * LANES:(j + 1) * LANES]
            ms = []
            for half in range(2):
                hd = off + 2 * j + half
                seg = cs[:, hd:hd + 1] - src_t[hd:hd + 1, :]
                ms.append((cb * jnp.exp(jnp.where(mask, seg, -jnp.inf))).astype(BF16))
            x_pair = jnp.concatenate([jnp.where(low, x_tile, zero), jnp.where(low, zero, x_tile)], axis=0)
            y_tiles.append(y_off[:, jj * LANES:(jj + 1) * LANES] + _dot(jnp.concatenate(ms, axis=1), x_pair))
        s_new = _dot_tn(bm, xdec[:, g * gw:(g + 1) * gw])
        h_ref[:, g * gw:(g + 1) * gw] = h_prev * e3[end:end + 1, g * gw:(g + 1) * gw] + s_new
    return jnp.concatenate(y_tiles, axis=1)


def _ssd_kernel(xf_ref, xb_ref, dtf_ref, dtb_ref, bias_ref, alog_ref, dskip_ref, sel_ref,
                yf_ref, yb_ref, hf_ref, hb_ref):
    c = pl.program_id(1)

    @pl.when(c == 0)
    def _():
        hf_ref[...] = jnp.zeros_like(hf_ref)
        hb_ref[...] = jnp.zeros_like(hb_ref)

    a_neg = -jnp.exp(alog_ref[...])
    bias = bias_ref[...]
    n_chunks = xf_ref.shape[1] // CHUNK
    dec_f = _ssd_decays(dtf_ref[0], bias, a_neg, sel_ref[0], reverse=False)
    dec_b = _ssd_decays(dtb_ref[0], bias, a_neg, sel_ref[1], reverse=True)

    def chunk(x_ref, dec, h_ref, c, reverse):
        cs, src_t, e2, e3 = dec
        rows = slice(c * CHUNK, (c + 1) * CHUNK)
        xc = x_ref[0, rows, :]
        return xc, _ssd_chunk(xc, cs[c], src_t[c], e2[rows], e3[rows], h_ref, reverse)

    for c in range(n_chunks):
        xf, yf = chunk(xf_ref, dec_f, hf_ref, c, False)
        yf_ref[0, c * CHUNK:(c + 1) * CHUNK, :] = (
            yf + xf[:, :SSM_INNER].astype(F32) * dskip_ref[...]).astype(BF16)
        cb = n_chunks - 1 - c
        _, yb = chunk(xb_ref, dec_b, hb_ref, cb, True)
        yb_ref[0, cb * CHUNK:(cb + 1) * CHUNK, :] = yb.astype(BF16)


def _ssd(xc, dt, bias, alog, dskip, n_chunks):
    b, l, _ = xc.shape
    rows = n_chunks * CHUNK
    nc = l // rows
    fwd = lambda n: pl.BlockSpec((1, rows, n), lambda bi, c: (bi, c, 0))
    bwd = lambda n: pl.BlockSpec((1, rows, n), lambda bi, c: (bi, nc - 1 - c, 0))
    lane_head = np.arange(SSM_INNER) // SSM_HEAD_DIM
    sel = np.stack([np.arange(DT_PAD)[:, None] == (d * SSM_HEADS + lane_head)[None, :] for d in range(2)])
    sel = jnp.asarray(np.concatenate([sel, sel], axis=1), BF16)
    return pl.pallas_call(
        _ssd_kernel,
        grid=(b, nc),
        in_specs=[fwd(CONV_DIM), bwd(CONV_DIM), fwd(DT_PAD), bwd(DT_PAD),
                  _const_spec(bias.shape), _const_spec(alog.shape), _const_spec(dskip.shape),
                  _const_spec(sel.shape)],
        out_specs=(fwd(SSM_INNER), bwd(SSM_INNER)),
        out_shape=(jax.ShapeDtypeStruct((b, l, SSM_INNER), BF16),
                   jax.ShapeDtypeStruct((b, l, SSM_INNER), BF16)),
        scratch_shapes=[pltpu.VMEM((SSM_STATE, SSM_INNER), F32),
                        pltpu.VMEM((SSM_STATE, SSM_INNER), F32)],
        compiler_params=_params("parallel", "arbitrary"),
        name="ssd",
    )(xc, xc, dt, dt, bias, alog, dskip, sel)


def _attn_kernel(bounded_ref, q_ref, k_ref, vt_ref, o_ref, m_ref, l_ref, acc_ref, *, tk):
    nk = k_ref.shape[1] // tk
    l_ref[...] = jnp.zeros_like(l_ref)
    acc_ref[...] = jnp.zeros_like(acc_ref)

    def tiles(i):
        ks = pl.multiple_of(i * tk, tk)
        return k_ref[0, pl.ds(ks, tk), :], vt_ref[0, :, pl.ds(ks, tk)]

    def bounded_body(i, carry):
        k, vt = tiles(i)
        for j in range(ATTN_GROUP):
            rows = slice(j * ATTN_HEAD_DIM, (j + 1) * ATTN_HEAD_DIM)
            p_t = jnp.exp2(_dot_nt(k, q_ref[0, :, j * LANES:(j + 1) * LANES]))
            part = p_t[0:SUBLANES, :]
            for r in range(1, tk // SUBLANES):
                part = part + p_t[r * SUBLANES:(r + 1) * SUBLANES, :]
            l_ref[j * SUBLANES:(j + 1) * SUBLANES, :] += part
            acc_ref[rows, :] += _dot(vt, p_t.astype(BF16))
        return carry

    def online_body(i, carry):
        k, vt = tiles(i)
        for j in range(ATTN_GROUP):
            rows = slice(j * ATTN_HEAD_DIM, (j + 1) * ATTN_HEAD_DIM)
            s_t = _dot_nt(k, q_ref[0, :, j * LANES:(j + 1) * LANES])
            m_prev = m_ref[j:j + 1, :]
            m_new = jnp.maximum(m_prev, jnp.max(s_t, axis=0, keepdims=True))
            alpha = jnp.exp2(m_prev - m_new)
            p_t = jnp.exp2(s_t - m_new)
            lrow = slice(j * SUBLANES, j * SUBLANES + 1)
            l_ref[lrow, :] = alpha * l_ref[lrow, :] + jnp.sum(p_t, axis=0, keepdims=True)
            acc_ref[rows, :] = alpha * acc_ref[rows, :] + _dot(vt, p_t.astype(BF16))
            m_ref[j:j + 1, :] = m_new
        return carry

    bounded = bounded_ref[0] != 0

    @pl.when(bounded)
    def _():
        lax.fori_loop(0, nk, bounded_body, 0)

    @pl.when(jnp.logical_not(bounded))
    def _():
        m_ref[...] = jnp.full_like(m_ref, -jnp.inf)
        lax.fori_loop(0, nk, online_body, 0)

    for j in range(ATTN_GROUP):
        rows = slice(j * ATTN_HEAD_DIM, (j + 1) * ATTN_HEAD_DIM)
        l = jnp.sum(l_ref[j * SUBLANES:(j + 1) * SUBLANES, :], axis=0, keepdims=True)
        acc_ref[rows, :] = acc_ref[rows, :] / l
    o_ref[0] = acc_ref[...].T


def _attention(bounded, qp, kp, vt, tq, tk):
    b, l, _ = qp.shape
    gw = ATTN_GROUP * LANES
    return pl.pallas_call(
        functools.partial(_attn_kernel, tk=tk),
        grid=(b, ATTN_KV_HEADS, l // tq),
        in_specs=[pl.BlockSpec(memory_space=pltpu.SMEM),
                  pl.BlockSpec((1, tq, gw), lambda bi, g, i: (bi, i, g)),
                  pl.BlockSpec((1, l, LANES), lambda bi, g, i: (bi, 0, 0)),
                  pl.BlockSpec((1, ATTN_HEAD_DIM, l), lambda bi, g, i: (bi, g, 0))],
        out_specs=pl.BlockSpec((1, tq, ATTN_GROUP * ATTN_HEAD_DIM), lambda bi, g, i: (bi, i, g)),
        out_shape=jax.ShapeDtypeStruct((b, l, ATTN_INNER), F32),
        scratch_shapes=[pltpu.VMEM((SUBLANES, tq), F32), pltpu.VMEM((ATTN_GROUP * SUBLANES, tq), F32),
                        pltpu.VMEM((ATTN_GROUP * ATTN_HEAD_DIM, tq), F32)],
        compiler_params=_params("parallel", "parallel", "parallel"),
        name="attention",
    )(bounded, qp, kp, vt)


def _mem_kv_kernel(mem_ref, g_ref, wk_ref, wv_ref, k_ref, v_ref):
    m = _rmsnorm(mem_ref[0], g_ref[...]).astype(BF16)
    k_ref[0] = _dot(m, wk_ref[...]).astype(BF16)
    v_ref[0] = _dot(m, wv_ref[...]).astype(BF16)


def _mem_kv(mem, g, wk, wv):
    b, n, d = mem.shape
    spec = pl.BlockSpec((1, n, X_INNER), lambda bi: (bi, 0, 0))
    return pl.pallas_call(
        _mem_kv_kernel,
        grid=(b,),
        in_specs=[pl.BlockSpec((1, n, d), lambda bi: (bi, 0, 0)), _const_spec(g.shape),
                  _const_spec(wk.shape), _const_spec(wv.shape)],
        out_specs=(spec, spec),
        out_shape=(jax.ShapeDtypeStruct((b, n, X_INNER), BF16),) * 2,
        compiler_params=_params("parallel"),
        name="mem_kv",
    )(mem, g, wk, wv)


def _mix_xattn_kernel(x_ref, yf_ref, yb_ref, z_ref, o_ref, kx_ref, vx_ref,
                      gssm_ref, gatt_ref, wout_ref, gpost_ref,
                      gxpre_ref, wxq_ref, wxo_ref, gxpost_ref, out_ref):
    tm = x_ref.shape[1]
    gw = SSM_INNER // SSM_GROUPS
    scale = X_HEAD_DIM ** -0.5

    def gate_norm(rows):
        z = z_ref[0, rows, :].astype(F32)
        y = (yf_ref[0, rows, :].astype(F32) + yb_ref[0, rows, :].astype(F32)) * (z * _sigmoid(z))
        yn = jnp.concatenate(
            [y[:, g * gw:(g + 1) * gw]
             * lax.rsqrt(jnp.mean(y[:, g * gw:(g + 1) * gw] * y[:, g * gw:(g + 1) * gw], axis=-1, keepdims=True)
                         + EPS)
             for g in range(SSM_GROUPS)], axis=1)
        return (yn * gssm_ref[...]).astype(BF16), _rmsnorm(o_ref[0, rows, :], gatt_ref[...]).astype(BF16)

    def mix_proj(ys):
        return _dot(ys[0], wout_ref[:SSM_INNER, :]) + _dot(ys[1], wout_ref[SSM_INNER:, :])

    def residual_prenorm(rows, mix):
        x1 = x_ref[0, rows, :] + _rmsnorm(mix, gpost_ref[...])
        return x1, _rmsnorm(x1, gxpre_ref[...]).astype(BF16)

    def attend(h):
        q = _dot(h, wxq_ref[...])
        heads = []
        for hh in range(X_HEADS):
            cols = slice(hh * X_HEAD_DIM, (hh + 1) * X_HEAD_DIM)
            s = _dot_nt(q[:, cols].astype(BF16), kx_ref[0, :, cols])
            e = jnp.exp2((s - jnp.max(s, axis=-1, keepdims=True)) * (scale * LOG2E))
            inv = 1.0 / jnp.sum(e, axis=-1, keepdims=True)
            heads.append(_dot(e.astype(BF16), vx_ref[0, :, cols]) * inv)
        return _dot(jnp.concatenate(heads, axis=1).astype(BF16), wxo_ref[...])

    halves = [slice(r * (tm // 2), (r + 1) * (tm // 2)) for r in range(2)]
    ys = [gate_norm(rows) for rows in halves]
    mixes = [mix_proj(y) for y in ys]
    xh = [residual_prenorm(rows, m) for rows, m in zip(halves, mixes)]
    cas = [attend(h) for _, h in xh]
    for rows, (x1, _), ca in zip(halves, xh, cas):
        out_ref[0, rows, :] = x1 + _rmsnorm(ca, gxpost_ref[...])


def _mix_xattn(x, yf, yb, z, o, kx, vx, gssm, gatt, wout, gpost, gxpre, wxq, wxo, gxpost, tm):
    b, l, d = x.shape
    n_mem = kx.shape[1]
    tok = lambda n: pl.BlockSpec((1, tm, n), lambda bi, i: (bi, i, 0))
    mem = pl.BlockSpec((1, n_mem, X_INNER), lambda bi, i: (bi, 0, 0))
    consts = (gssm, gatt, wout, gpost, gxpre, wxq, wxo, gxpost)
    return pl.pallas_call(
        _mix_xattn_kernel,
        grid=(b, l // tm),
        in_specs=[tok(d), tok(SSM_INNER), tok(SSM_INNER), tok(SSM_INNER), tok(ATTN_INNER), mem, mem]
                 + [_const_spec(c.shape) for c in consts],
        out_specs=tok(d),
        out_shape=jax.ShapeDtypeStruct((b, l, d), F32),
        compiler_params=_params("parallel", "parallel"),
        name="mix_xattn",
    )(x, yf, yb, z, o, kx, vx, *consts)


def _ffn_kernel(x_ref, gpre_ref, wg_ref, wu_ref, wd_ref, gpost_ref, out_ref):
    tm = x_ref.shape[1]
    halves = [slice(r * (tm // 2), (r + 1) * (tm // 2)) for r in range(2)]
    xs = [x_ref[0, rows, :] for rows in halves]
    hs = [_rmsnorm(x, gpre_ref[...]).astype(BF16) for x in xs]
    gu = [(_dot(h, wg_ref[...]), _dot(h, wu_ref[...])) for h in hs]
    acts = [(gt * _sigmoid(gt) * up).astype(BF16) for gt, up in gu]
    fs = [_dot(a, wd_ref[...]) for a in acts]
    for rows, x, f in zip(halves, xs, fs):
        out_ref[0, rows, :] = x + _rmsnorm(f, gpost_ref[...])


def _ffn(x, gpre, wg, wu, wd, gpost, tm):
    b, l, d = x.shape
    tok = pl.BlockSpec((1, tm, d), lambda bi, i: (bi, i, 0))
    single = lambda a: pl.BlockSpec(a.shape, lambda *_: (0,) * a.ndim, pipeline_mode=pl.Buffered(1))
    return pl.pallas_call(
        _ffn_kernel,
        grid=(b, l // tm),
        in_specs=[tok, _const_spec(gpre.shape), single(wg), single(wu), single(wd), _const_spec(gpost.shape)],
        out_specs=tok,
        out_shape=jax.ShapeDtypeStruct((b, l, d), F32),
        compiler_params=_params("parallel", "parallel"),
        name="ffn",
    )(x, gpre, wg, wu, wd, gpost)


def _rope_tables(l):
    t = jnp.arange(l, dtype=jnp.int32)
    r = (t // GRID_W).astype(F32)
    c = (t % GRID_W).astype(F32)
    half = ATTN_HEAD_DIM // 2
    inv = 1.0 / (ROPE_THETA ** (jnp.arange(0, half, 2, dtype=F32) / half))
    ang_r = r[:, None] * inv
    ang_c = c[:, None] * inv
    cos = jnp.concatenate([jnp.cos(ang_r)] * 2 + [jnp.cos(ang_c)] * 2, axis=1)
    sin = jnp.concatenate([-jnp.sin(ang_r), jnp.sin(ang_r), -jnp.sin(ang_c), jnp.sin(ang_c)], axis=1)
    return jnp.tile(cos, (1, 2)), jnp.tile(sin, (1, 2))


def _tile(l, want):
    t = min(l, want)
    assert l % t == 0
    return t


def _layer(x, mem, w):
    b, l, _ = x.shape
    assert l % CHUNK == 0 and l % GRID_W == 0
    tm = _tile(l, 512)
    cos, sin = _rope_tables(l)
    z, xc, dt, qp, kp, vt = _in_proj(x, w["g_pre"], w["wxbc"], w["wrest"], cos, sin, w["qg"], w["kg"],
                                     w["conv_w"], w["conv_b"], tm)
    yf, yb = _ssd(xc, dt, w["dt_bias"], w["a_log"], w["d_skip"], _tile(l, 8 * CHUNK) // CHUNK)
    o = _attention(w["attn_bounded"], qp, kp, vt, _tile(l, 1024), _tile(l, 2048))
    kx, vx = _mem_kv(mem, w["g_mem"], w["wxk"], w["wxv"])
    x2 = _mix_xattn(x, yf, yb, z, o, kx, vx, w["g_ssm"], w["g_att"], w["wout"], w["g_post"],
                    w["g_xpre"], w["wxq"], w["wxo"], w["g_xpost"], _tile(l, 1024))
    return _ffn(x2, w["g_fpre"], w["wg"], w["wu"], w["wd"], w["g_fpost"], tm)


def _prep_weights(i, norm_mix_pre, w_in, conv_w, conv_b, dt_bias, a_log, d_skip, ssm_norm, q_norm, k_norm,
                  attn_norm, w_out, norm_mix_post, norm_x_pre, norm_mem, w_xq, w_xk, w_xv, w_xo, norm_x_post,
                  norm_ffn_pre, w_gate, w_up, w_down, norm_ffn_post):
    row = lambda v: v.reshape(1, -1).astype(F32)
    splits = np.cumsum([SSM_INNER, CONV_DIM, 2 * SSM_HEADS, ATTN_INNER,
                        ATTN_KV_HEADS * ATTN_HEAD_DIM, ATTN_KV_HEADS * ATTN_HEAD_DIM])[:-1]
    wz, wxbc, wdt, wq, wk, wv = jnp.split(w_in[i], [int(s) for s in splits], axis=-1)
    order = [h + ATTN_GROUP * g for h in range(ATTN_GROUP) for g in range(ATTN_KV_HEADS)]
    wq = wq.reshape(D_MODEL, ATTN_HEADS, ATTN_HEAD_DIM)[:, order, :].reshape(D_MODEL, ATTN_INNER)
    pad = DT_PAD - 2 * SSM_HEADS
    q_gain = row(q_norm[i]) * (ATTN_HEAD_DIM ** -0.5 * np.log2(np.e))
    k_gain = row(k_norm[i])
    score_bound = ATTN_HEAD_DIM * jnp.max(jnp.abs(q_gain)) * jnp.max(jnp.abs(k_gain))
    return {
        "attn_bounded": (score_bound * BOUND_MARGIN <= MAX_EXP2_SCORE).astype(jnp.int32).reshape(1),
        "g_pre": row(norm_mix_pre[i]),
        "wxbc": wxbc.astype(BF16),
        "wrest": jnp.concatenate([wz, wq, wk, wv, jnp.pad(wdt, ((0, 0), (0, pad)))], axis=1).astype(BF16),
        "qg": jnp.tile(q_gain, (1, LANES // ATTN_HEAD_DIM)),
        "kg": jnp.tile(k_gain, (1, LANES // ATTN_HEAD_DIM)),
        "conv_w": conv_w[i].astype(F32), "conv_b": row(conv_b[i]),
        "dt_bias": jnp.pad(row(dt_bias[i]), ((0, 0), (0, pad))),
        "a_log": jnp.pad(row(a_log[i]), ((0, 0), (0, pad))),
        "d_skip": jnp.repeat(row(d_skip[i]), SSM_HEAD_DIM, axis=1),
        "g_ssm": row(ssm_norm[i]), "g_att": row(attn_norm[i]),
        "wout": w_out[i].astype(BF16),
        "g_post": row(norm_mix_post[i]), "g_xpre": row(norm_x_pre[i]), "g_mem": row(norm_mem[i]),
        "wxq": w_xq[i].astype(BF16), "wxk": w_xk[i].astype(BF16), "wxv": w_xv[i].astype(BF16),
        "wxo": w_xo[i].astype(BF16), "g_xpost": row(norm_x_post[i]),
        "g_fpre": row(norm_ffn_pre[i]), "wg": w_gate[i].astype(BF16), "wu": w_up[i].astype(BF16),
        "wd": w_down[i].astype(BF16), "g_fpost": row(norm_ffn_post[i]),
    }


def kernel(x_prompt, x_sample, mem_prompt, mem_sample, norm_mix_pre, w_in, conv_w, conv_b, dt_bias, a_log,
           d_skip, ssm_norm, q_norm, k_norm, attn_norm, w_out, norm_mix_post, norm_x_pre, norm_mem, w_xq,
           w_xk, w_xv, w_xo, norm_x_post, norm_ffn_pre, w_gate, w_up, w_down, norm_ffn_post):
    y_prompt, y_sample = x_prompt, x_sample
    for i in range(w_in.shape[0]):
        w = _prep_weights(i, norm_mix_pre, w_in, conv_w, conv_b, dt_bias, a_log, d_skip, ssm_norm, q_norm,
                          k_norm, attn_norm, w_out, norm_mix_post, norm_x_pre, norm_mem, w_xq, w_xk, w_xv,
                          w_xo, norm_x_post, norm_ffn_pre, w_gate, w_up, w_down, norm_ffn_post)
        y_prompt = _layer(y_prompt, mem_prompt, w)
        y_sample = _layer(y_sample, mem_sample, w)
    return (y_prompt, y_sample)
```

```python
import functools

import numpy as np
import jax
import jax.numpy as jnp
from jax import lax
from jax.experimental import pallas as pl
from jax.experimental.pallas import tpu as pltpu

F32 = jnp.float32
BF16 = jnp.bfloat16

EPS = 1e-6
LANES = 128
SUBLANES = 8
VMEM_LIMIT_BYTES = 56 * 1024 * 1024

D_MODEL = 1024
SSM_HEADS = 16
SSM_HEAD_DIM = 64
SSM_INNER = SSM_HEADS * SSM_HEAD_DIM
SSM_GROUPS = 2
SSM_STATE = 128
D_CONV = 5
CHUNK = 128
CONV_DIM = SSM_INNER + 2 * SSM_GROUPS * SSM_STATE
ATTN_HEADS = 8
ATTN_KV_HEADS = 2
ATTN_HEAD_DIM = 64
ATTN_INNER = ATTN_HEADS * ATTN_HEAD_DIM
ATTN_GROUP = ATTN_HEADS // ATTN_KV_HEADS
GRID_W = 64
ROPE_THETA = 10000.0
X_HEADS = 4
X_HEAD_DIM = 128
X_INNER = X_HEADS * X_HEAD_DIM
DT_PAD = LANES
HALO = SUBLANES
MAX_EXP2_SCORE = 60.0
BOUND_MARGIN = 1.02
Q_STRIP = 512


def _dot(a, b):
    return jnp.dot(a, b, preferred_element_type=F32)


def _dot_nt(a, b):
    return lax.dot_general(a, b, (((1,), (1,)), ((), ())), preferred_element_type=F32)


def _dot_tn(a, b):
    return lax.dot_general(a, b, (((0,), (0,)), ((), ())), preferred_element_type=F32)


def _rmsnorm(x, g):
    return x * lax.rsqrt(jnp.mean(x * x, axis=-1, keepdims=True) + EPS) * g


LOG2E = float(np.log2(np.e))


def _sigmoid(x):
    return 1.0 / (1.0 + jnp.exp2(x * -LOG2E))


def _params(*sem):
    return pltpu.CompilerParams(dimension_semantics=sem, vmem_limit_bytes=VMEM_LIMIT_BYTES)


def _const_spec(shape):
    nd = len(shape)
    return pl.BlockSpec(shape, lambda *_: (0,) * nd)


def _in_proj_kernel(x_ref, xp_ref, xn_ref, g_ref, wxbc_ref, wrest_ref,
                    cos_ref, sin_ref, qg_ref, kg_ref, cw_ref, cb_ref,
                    z_ref, xc_ref, dt_ref, qp_ref, kp_ref, vt_ref, ext_ref, conv_ref):
    i = pl.program_id(1)
    n = pl.num_programs(1)
    tm = x_ref.shape[1]
    x_all = jnp.concatenate([x_ref[0], xp_ref[0], xn_ref[0]], axis=0)
    h_all = _rmsnorm(x_all, g_ref[...]).astype(BF16)
    h = h_all[:tm]

    xbc = _dot(h_all, wxbc_ref[...])
    for s in range(CONV_DIM // LANES):
        cols = slice(s * LANES, (s + 1) * LANES)
        ext_ref[s, 0:HALO, :] = jnp.where(i > 0, xbc[tm:tm + HALO, cols], 0.0)
        ext_ref[s, HALO:HALO + tm, :] = xbc[:tm, cols]
        ext_ref[s, HALO + tm:HALO + tm + HALO, :] = jnp.where(i < n - 1, xbc[tm + HALO:, cols], 0.0)
        ext_ref[s, HALO + tm + HALO:, :] = jnp.zeros((HALO, LANES), F32)
    pitch = tm // SUBLANES + 1
    for s in range(CONV_DIM // LANES):
        cols = slice(s * LANES, (s + 1) * LANES)
        taps = [cw_ref[j:j + 1, cols] for j in range(D_CONV)]
        bias = cb_ref[:, cols]
        for a in range(pitch):
            acc = bias
            for j in range(D_CONV):
                start = HALO - D_CONV // 2 + j + a
                acc = acc + ext_ref[s, pl.ds(start, SUBLANES, stride=pitch), :] * taps[j]
            conv_ref[s, pl.ds(a, SUBLANES, stride=pitch), :] = acc * _sigmoid(acc)
        xc_ref[0, :, cols] = conv_ref[s, 0:tm, :].astype(BF16)

    rest = _dot(h, wrest_ref[...])
    z_ref[0] = rest[:, :SSM_INNER].astype(BF16)
    q = rest[:, SSM_INNER:SSM_INNER + ATTN_INNER]
    k = rest[:, SSM_INNER + ATTN_INNER:SSM_INNER + ATTN_INNER + LANES]
    v = rest[:, SSM_INNER + ATTN_INNER + LANES:SSM_INNER + ATTN_INNER + 2 * LANES]
    dt_ref[0] = rest[:, SSM_INNER + ATTN_INNER + 2 * LANES:]
    vt_ref[0] = v.T.astype(BF16)

    cos = cos_ref[...]
    sin = sin_ref[...]
    lane = lax.broadcasted_iota(jnp.int32, cos.shape, 1)

    r_id = (lax.broadcasted_iota(jnp.int32, (2 * LANES, LANES), 0) % LANES) // ATTN_HEAD_DIM
    c_id = lax.broadcasted_iota(jnp.int32, (2 * LANES, LANES), 1) // ATTN_HEAD_DIM
    same_head = jnp.where(r_id == c_id, 1.0, 0.0).astype(BF16)

    def head_norm_rope(blk, gain):
        sq = blk * blk
        hi = sq.astype(BF16)
        lo = (sq - hi.astype(F32)).astype(BF16)
        ss = _dot(jnp.concatenate([hi, lo], axis=1), same_head)
        y = blk * lax.rsqrt(ss * (1.0 / ATTN_HEAD_DIM) + EPS) * gain
        partner = jnp.where((lane & 16) != 0, pltpu.roll(y, 16, 1), pltpu.roll(y, LANES - 16, 1))
        return y * cos + partner * sin

    kp_ref[0] = head_norm_rope(k, kg_ref[...]).astype(BF16)
    qg = qg_ref[...]
    zeros = jnp.zeros((ATTN_HEAD_DIM, tm), BF16)
    for j in range(ATTN_GROUP):
        r_t = head_norm_rope(q[:, j * LANES:(j + 1) * LANES], qg).T.astype(BF16)
        qp_ref[0, j * LANES:(j + 1) * LANES, :] = jnp.concatenate([r_t[:ATTN_HEAD_DIM], zeros], axis=0)
        qp_ref[0, (ATTN_GROUP + j) * LANES:(ATTN_GROUP + j + 1) * LANES, :] = jnp.concatenate(
            [zeros, r_t[ATTN_HEAD_DIM:]], axis=0)


def _in_proj(x, g, wxbc, wrest, cos, sin, qg, kg, cw, cb, tm):
    b, l, d = x.shape
    grid = (b, l // tm)
    nh = tm // HALO
    last = l // HALO - 1
    tok = lambda n: pl.BlockSpec((1, tm, n), lambda bi, i: (bi, i, 0))
    pos = pl.BlockSpec((tm, LANES), lambda bi, i: (i, 0))
    prev_rows = pl.BlockSpec((1, HALO, d), lambda bi, i: (bi, jnp.maximum(i * nh - 1, 0), 0))
    next_rows = pl.BlockSpec((1, HALO, d), lambda bi, i: (bi, jnp.minimum((i + 1) * nh, last), 0))
    out_shape = (
        jax.ShapeDtypeStruct((b, l, SSM_INNER), BF16),
        jax.ShapeDtypeStruct((b, l, CONV_DIM), BF16),
        jax.ShapeDtypeStruct((b, l, DT_PAD), F32),
        jax.ShapeDtypeStruct((b, ATTN_HEADS * LANES, l), BF16),
        jax.ShapeDtypeStruct((b, l, LANES), BF16),
        jax.ShapeDtypeStruct((b, LANES, l), BF16),
    )
    tok_t = lambda n: pl.BlockSpec((1, n, tm), lambda bi, i: (bi, 0, i))
    return pl.pallas_call(
        _in_proj_kernel,
        grid=grid,
        in_specs=[tok(d), prev_rows, next_rows, _const_spec(g.shape), _const_spec(wxbc.shape),
                  _const_spec(wrest.shape), pos, pos, _const_spec(qg.shape), _const_spec(kg.shape),
                  _const_spec(cw.shape), _const_spec(cb.shape)],
        out_specs=(tok(SSM_INNER), tok(CONV_DIM), tok(DT_PAD), tok_t(ATTN_HEADS * LANES), tok(LANES),
                   tok_t(LANES)),
        out_shape=out_shape,
        scratch_shapes=[pltpu.VMEM((CONV_DIM // LANES, tm + 3 * HALO, LANES), F32),
                        pltpu.VMEM((CONV_DIM // LANES, tm + SUBLANES, LANES), F32)],
        compiler_params=_params("parallel", "parallel"),
        name="in_proj",
    )(x, x, x, g, wxbc, wrest, cos, sin, qg, kg, cw, cb)


def _expand_heads(w, sel):
    hi = w.astype(BF16)
    lo = (w - hi.astype(F32)).astype(BF16)
    return _dot(jnp.concatenate([hi, lo], axis=1), sel)


def _chunk_mask(reverse):
    row = lax.broadcasted_iota(jnp.int32, (CHUNK, CHUNK), 0)
    col = lax.broadcasted_iota(jnp.int32, (CHUNK, CHUNK), 1)
    return (row <= col) if reverse else (row >= col)


def _ssd_decays(dt_raw, dtb, a_neg, sel, reverse):
    n_chunks = dt_raw.shape[0] // CHUNK
    tri = _chunk_mask(reverse).astype(F32)
    end = 0 if reverse else CHUNK - 1
    xpre = dt_raw + dtb
    dtv = jnp.maximum(xpre, 0.0) + jnp.log1p(jnp.exp(-jnp.abs(xpre)))
    a = dtv * a_neg
    cs_chunks = [jnp.dot(tri, a[c * CHUNK:(c + 1) * CHUNK], preferred_element_type=F32,
                         precision=lax.Precision.HIGHEST) for c in range(n_chunks)]
    cs = jnp.concatenate(cs_chunks, axis=0)
    cs_end = jnp.concatenate([jnp.broadcast_to(c[end:end + 1, :], (CHUNK, LANES)) for c in cs_chunks], axis=0)
    e3 = _expand_heads(jnp.exp(cs), sel)
    e2 = _expand_heads(dtv * jnp.exp(cs_end - cs), sel)
    src_t = [(c - jnp.log(dtv[i * CHUNK:(i + 1) * CHUNK])).T for i, c in enumerate(cs_chunks)]
    return cs_chunks, src_t, e2, e3


def _ssd_chunk(xc, cs, src_t, e2, e3, h_ref, reverse):
    off = SSM_HEADS if reverse else 0
    mask = _chunk_mask(reverse)
    end = 0 if reverse else CHUNK - 1
    xs = xc[:, :SSM_INNER]
    xdec = (xs.astype(F32) * e2).astype(BF16)
    lane = lax.broadcasted_iota(jnp.int32, (CHUNK, LANES), 1)
    low = lane < SSM_HEAD_DIM
    zero = jnp.zeros((CHUNK, LANES), BF16)
    gw = SSM_INNER // SSM_GROUPS
    hpg = SSM_HEADS // SSM_GROUPS
    y_tiles = []
    for g in range(SSM_GROUPS):
        bm = xc[:, SSM_INNER + g * SSM_STATE:SSM_INNER + (g + 1) * SSM_STATE]
        cm = xc[:, SSM_INNER + (SSM_GROUPS + g) * SSM_STATE:SSM_INNER + (SSM_GROUPS + g + 1) * SSM_STATE]
        cb = _dot_nt(cm, bm)
        h_prev = h_ref[:, g * gw:(g + 1) * gw]
        y_off = _dot(cm, h_prev.astype(BF16)) * e3[:, g * gw:(g + 1) * gw]
        for jj in range(hpg // 2):
            j = g * (hpg // 2) + jj
            x_tile = xs[:, j * LANES:(j + 1) * LANES]
            ms = []
            for half in range(2):
                hd = off + 2 * j + half
                seg = cs[:, hd:hd + 1] - src_t[hd:hd + 1, :]
                ms.append((cb * jnp.exp(jnp.where(mask, seg, -jnp.inf))).astype(BF16))
            x_pair = jnp.concatenate([jnp.where(low, x_tile, zero), jnp.where(low, zero, x_tile)], axis=0)
            y_tiles.append(y_off[:, jj * LANES:(jj + 1) * LANES] + _dot(jnp.concatenate(ms, axis=1), x_pair))
        s_new = _dot_tn(bm, xdec[:, g * gw:(g + 1) * gw])
        h_ref[:, g * gw:(g + 1) * gw] = h_prev * e3[end:end + 1, g * gw:(g + 1) * gw] + s_new
    return jnp.concatenate(y_tiles, axis=1)


def _ssd_kernel(xf_ref, xb_ref, dtf_ref, dtb_ref, bias_ref, alog_ref, dskip_ref, sel_ref,
                yf_ref, yb_ref, hf_ref, hb_ref):
    c = pl.program_id(1)

    @pl.when(c == 0)
    def _():
        hf_ref[...] = jnp.zeros_like(hf_ref)
        hb_ref[...] = jnp.zeros_like(hb_ref)

    a_neg = -jnp.exp(alog_ref[...])
    bias = bias_ref[...]
    n_chunks = xf_ref.shape[1] // CHUNK
    dec_f = _ssd_decays(dtf_ref[0], bias, a_neg, sel_ref[0], reverse=False)
    dec_b = _ssd_decays(dtb_ref[0], bias, a_neg, sel_ref[1], reverse=True)

    def chunk(x_ref, dec, h_ref, c, reverse):
        cs, src_t, e2, e3 = dec
        rows = slice(c * CHUNK, (c + 1) * CHUNK)
        xc = x_ref[0, rows, :]
        return xc, _ssd_chunk(xc, cs[c], src_t[c], e2[rows], e3[rows], h_ref, reverse)

    for c in range(n_chunks):
        xf, yf = chunk(xf_ref, dec_f, hf_ref, c, False)
        yf_ref[0, c * CHUNK:(c + 1) * CHUNK, :] = (
            yf + xf[:, :SSM_INNER].astype(F32) * dskip_ref[...]).astype(BF16)
        cb = n_chunks - 1 - c
        _, yb = chunk(xb_ref, dec_b, hb_ref, cb, True)
        yb_ref[0, cb * CHUNK:(cb + 1) * CHUNK, :] = yb.astype(BF16)


def _ssd(xc, dt, bias, alog, dskip, n_chunks):
    b, l, _ = xc.shape
    rows = n_chunks * CHUNK
    nc = l // rows
    fwd = lambda n: pl.BlockSpec((1, rows, n), lambda bi, c: (bi, c, 0))
    bwd = lambda n: pl.BlockSpec((1, rows, n), lambda bi, c: (bi, nc - 1 - c, 0))
    lane_head = np.arange(SSM_INNER) // SSM_HEAD_DIM
    sel = np.stack([np.arange(DT_PAD)[:, None] == (d * SSM_HEADS + lane_head)[None, :] for d in range(2)])
    sel = jnp.asarray(np.concatenate([sel, sel], axis=1), BF16)
    return pl.pallas_call(
        _ssd_kernel,
        grid=(b, nc),
        in_specs=[fwd(CONV_DIM), bwd(CONV_DIM), fwd(DT_PAD), bwd(DT_PAD),
                  _const_spec(bias.shape), _const_spec(alog.shape), _const_spec(dskip.shape),
                  _const_spec(sel.shape)],
        out_specs=(fwd(SSM_INNER), bwd(SSM_INNER)),
        out_shape=(jax.ShapeDtypeStruct((b, l, SSM_INNER), BF16),
                   jax.ShapeDtypeStruct((b, l, SSM_INNER), BF16)),
        scratch_shapes=[pltpu.VMEM((SSM_STATE, SSM_INNER), F32),
                        pltpu.VMEM((SSM_STATE, SSM_INNER), F32)],
        compiler_params=_params("parallel", "arbitrary"),
        name="ssd",
    )(xc, xc, dt, dt, bias, alog, dskip, sel)


def _attn_kernel(bounded_ref, q_ref, k_ref, vt_ref, o_ref, m_ref, l_ref, acc_ref, *, tk):
    nk = k_ref.shape[1] // tk
    l_ref[...] = jnp.zeros_like(l_ref)
    acc_ref[...] = jnp.zeros_like(acc_ref)

    def tiles(i):
        ks = pl.multiple_of(i * tk, tk)
        return k_ref[0, pl.ds(ks, tk), :], vt_ref[0, :, pl.ds(ks, tk)]

    tq = q_ref.shape[2]
    strip = min(tq, Q_STRIP)

    def bounded_body(i, carry):
        k, vt = tiles(i)
        for j in range(ATTN_GROUP):
            rows = slice(j * ATTN_HEAD_DIM, (j + 1) * ATTN_HEAD_DIM)
            for c in range(tq // strip):
                cols = slice(c * strip, (c + 1) * strip)
                p_t = jnp.exp2(_dot(k, q_ref[0, j * LANES:(j + 1) * LANES, cols]))
                part = p_t[0:SUBLANES, :]
                for r in range(1, tk // SUBLANES):
                    part = part + p_t[r * SUBLANES:(r + 1) * SUBLANES, :]
                l_ref[j * SUBLANES:(j + 1) * SUBLANES, cols] += part
                acc_ref[rows, cols] += _dot(vt, p_t.astype(BF16))
        return carry

    def online_body(i, carry):
        k, vt = tiles(i)
        for j in range(ATTN_GROUP):
            rows = slice(j * ATTN_HEAD_DIM, (j + 1) * ATTN_HEAD_DIM)
            s_t = _dot(k, q_ref[0, j * LANES:(j + 1) * LANES, :])
            m_prev = m_ref[j:j + 1, :]
            m_new = jnp.maximum(m_prev, jnp.max(s_t, axis=0, keepdims=True))
            alpha = jnp.exp2(m_prev - m_new)
            p_t = jnp.exp2(s_t - m_new)
            lrow = slice(j * SUBLANES, j * SUBLANES + 1)
            l_ref[lrow, :] = alpha * l_ref[lrow, :] + jnp.sum(p_t, axis=0, keepdims=True)
            acc_ref[rows, :] = alpha * acc_ref[rows, :] + _dot(vt, p_t.astype(BF16))
            m_ref[j:j + 1, :] = m_new
        return carry

    bounded = bounded_ref[0] != 0

    @pl.when(bounded)
    def _():
        lax.fori_loop(0, nk, bounded_body, 0)

    @pl.when(jnp.logical_not(bounded))
    def _():
        m_ref[...] = jnp.full_like(m_ref, -jnp.inf)
        lax.fori_loop(0, nk, online_body, 0)

    for j in range(ATTN_GROUP):
        rows = slice(j * ATTN_HEAD_DIM, (j + 1) * ATTN_HEAD_DIM)
        l = jnp.sum(l_ref[j * SUBLANES:(j + 1) * SUBLANES, :], axis=0, keepdims=True)
        acc_ref[rows, :] = acc_ref[rows, :] / l
    o_ref[0] = acc_ref[...].T


def _attention(bounded, qp, kp, vt, tq, tk):
    b, _, l = qp.shape
    gw = ATTN_GROUP * LANES
    return pl.pallas_call(
        functools.partial(_attn_kernel, tk=tk),
        grid=(b, ATTN_KV_HEADS, l // tq),
        in_specs=[pl.BlockSpec(memory_space=pltpu.SMEM),
                  pl.BlockSpec((1, gw, tq), lambda bi, g, i: (bi, g, i)),
                  pl.BlockSpec((1, l, LANES), lambda bi, g, i: (bi, 0, 0)),
                  pl.BlockSpec((1, ATTN_HEAD_DIM, l), lambda bi, g, i: (bi, g, 0))],
        out_specs=pl.BlockSpec((1, tq, ATTN_GROUP * ATTN_HEAD_DIM), lambda bi, g, i: (bi, i, g)),
        out_shape=jax.ShapeDtypeStruct((b, l, ATTN_INNER), F32),
        scratch_shapes=[pltpu.VMEM((SUBLANES, tq), F32), pltpu.VMEM((ATTN_GROUP * SUBLANES, tq), F32),
                        pltpu.VMEM((ATTN_GROUP * ATTN_HEAD_DIM, tq), F32)],
        compiler_params=_params("parallel", "parallel", "parallel"),
        name="attention",
    )(bounded, qp, kp, vt)


def _mem_kv_kernel(mem_ref, g_ref, wk_ref, wv_ref, k_ref, v_ref):
    m = _rmsnorm(mem_ref[0], g_ref[...]).astype(BF16)
    k_ref[0] = _dot(m, wk_ref[...]).astype(BF16)
    v_ref[0] = _dot(m, wv_ref[...]).astype(BF16)


def _mem_kv(mem, g, wk, wv):
    b, n, d = mem.shape
    spec = pl.BlockSpec((1, n, X_INNER), lambda bi: (bi, 0, 0))
    return pl.pallas_call(
        _mem_kv_kernel,
        grid=(b,),
        in_specs=[pl.BlockSpec((1, n, d), lambda bi: (bi, 0, 0)), _const_spec(g.shape),
                  _const_spec(wk.shape), _const_spec(wv.shape)],
        out_specs=(spec, spec),
        out_shape=(jax.ShapeDtypeStruct((b, n, X_INNER), BF16),) * 2,
        compiler_params=_params("parallel"),
        name="mem_kv",
    )(mem, g, wk, wv)


def _mix_xattn_kernel(x_ref, yf_ref, yb_ref, z_ref, o_ref, kx_ref, vx_ref,
                      gssm_ref, gatt_ref, wout_ref, gpost_ref,
                      gxpre_ref, wxq_ref, wxo_ref, gxpost_ref, out_ref):
    tm = x_ref.shape[1]
    gw = SSM_INNER // SSM_GROUPS
    scale = X_HEAD_DIM ** -0.5

    def gate_norm(rows):
        z = z_ref[0, rows, :].astype(F32)
        y = (yf_ref[0, rows, :].astype(F32) + yb_ref[0, rows, :].astype(F32)) * (z * _sigmoid(z))
        yn = jnp.concatenate(
            [y[:, g * gw:(g + 1) * gw]
             * lax.rsqrt(jnp.mean(y[:, g * gw:(g + 1) * gw] * y[:, g * gw:(g + 1) * gw], axis=-1, keepdims=True)
                         + EPS)
             for g in range(SSM_GROUPS)], axis=1)
        return (yn * gssm_ref[...]).astype(BF16), _rmsnorm(o_ref[0, rows, :], gatt_ref[...]).astype(BF16)

    def mix_proj(ys):
        return _dot(ys[0], wout_ref[:SSM_INNER, :]) + _dot(ys[1], wout_ref[SSM_INNER:, :])

    def residual_prenorm(rows, mix):
        x1 = x_ref[0, rows, :] + _rmsnorm(mix, gpost_ref[...])
        return x1, _rmsnorm(x1, gxpre_ref[...]).astype(BF16)

    def attend(h):
        q = _dot(h, wxq_ref[...])
        heads = []
        for hh in range(X_HEADS):
            cols = slice(hh * X_HEAD_DIM, (hh + 1) * X_HEAD_DIM)
            s = _dot_nt(q[:, cols].astype(BF16), kx_ref[0, :, cols])
            e = jnp.exp2((s - jnp.max(s, axis=-1, keepdims=True)) * (scale * LOG2E))
            inv = 1.0 / jnp.sum(e, axis=-1, keepdims=True)
            heads.append(_dot(e.astype(BF16), vx_ref[0, :, cols]) * inv)
        return _dot(jnp.concatenate(heads, axis=1).astype(BF16), wxo_ref[...])

    halves = [slice(r * (tm // 2), (r + 1) * (tm // 2)) for r in range(2)]
    ys = [gate_norm(rows) for rows in halves]
    mixes = [mix_proj(y) for y in ys]
    xh = [residual_prenorm(rows, m) for rows, m in zip(halves, mixes)]
    cas = [attend(h) for _, h in xh]
    for rows, (x1, _), ca in zip(halves, xh, cas):
        out_ref[0, rows, :] = x1 + _rmsnorm(ca, gxpost_ref[...])


def _mix_xattn(x, yf, yb, z, o, kx, vx, gssm, gatt, wout, gpost, gxpre, wxq, wxo, gxpost, tm):
    b, l, d = x.shape
    n_mem = kx.shape[1]
    tok = lambda n: pl.BlockSpec((1, tm, n), lambda bi, i: (bi, i, 0))
    mem = pl.BlockSpec((1, n_mem, X_INNER), lambda bi, i: (bi, 0, 0))
    consts = (gssm, gatt, wout, gpost, gxpre, wxq, wxo, gxpost)
    return pl.pallas_call(
        _mix_xattn_kernel,
        grid=(b, l // tm),
        in_specs=[tok(d), tok(SSM_INNER), tok(SSM_INNER), tok(SSM_INNER), tok(ATTN_INNER), mem, mem]
                 + [_const_spec(c.shape) for c in consts],
        out_specs=tok(d),
        out_shape=jax.ShapeDtypeStruct((b, l, d), F32),
        compiler_params=_params("parallel", "parallel"),
        name="mix_xattn",
    )(x, yf, yb, z, o, kx, vx, *consts)


def _ffn_kernel(x_ref, gpre_ref, wg_ref, wu_ref, wd_ref, gpost_ref, out_ref):
    tm = x_ref.shape[1]
    halves = [slice(r * (tm // 2), (r + 1) * (tm // 2)) for r in range(2)]
    xs = [x_ref[0, rows, :] for rows in halves]
    hs = [_rmsnorm(x, gpre_ref[...]).astype(BF16) for x in xs]
    gu = [(_dot(h, wg_ref[...]), _dot(h, wu_ref[...])) for h in hs]
    acts = [(gt * _sigmoid(gt) * up).astype(BF16) for gt, up in gu]
    fs = [_dot(a, wd_ref[...]) for a in acts]
    for rows, x, f in zip(halves, xs, fs):
        out_ref[0, rows, :] = x + _rmsnorm(f, gpost_ref[...])


def _ffn(x, gpre, wg, wu, wd, gpost, tm):
    b, l, d = x.shape
    tok = pl.BlockSpec((1, tm, d), lambda bi, i: (bi, i, 0))
    single = lambda a: pl.BlockSpec(a.shape, lambda *_: (0,) * a.ndim, pipeline_mode=pl.Buffered(1))
    return pl.pallas_call(
        _ffn_kernel,
        grid=(b, l // tm),
        in_specs=[tok, _const_spec(gpre.shape), single(wg), single(wu), single(wd), _const_spec(gpost.shape)],
        out_specs=tok,
        out_shape=jax.ShapeDtypeStruct((b, l, d), F32),
        compiler_params=_params("parallel", "parallel"),
        name="ffn",
    )(x, gpre, wg, wu, wd, gpost)


def _rope_tables(l):
    t = jnp.arange(l, dtype=jnp.int32)
    r = (t // GRID_W).astype(F32)
    c = (t % GRID_W).astype(F32)
    half = ATTN_HEAD_DIM // 2
    inv = 1.0 / (ROPE_THETA ** (jnp.arange(0, half, 2, dtype=F32) / half))
    ang_r = r[:, None] * inv
    ang_c = c[:, None] * inv
    cos = jnp.concatenate([jnp.cos(ang_r)] * 2 + [jnp.cos(ang_c)] * 2, axis=1)
    sin = jnp.concatenate([-jnp.sin(ang_r), jnp.sin(ang_r), -jnp.sin(ang_c), jnp.sin(ang_c)], axis=1)
    return jnp.tile(cos, (1, 2)), jnp.tile(sin, (1, 2))


def _tile(l, want):
    t = min(l, want)
    assert l % t == 0
    return t


def _layer(x, mem, w):
    b, l, _ = x.shape
    assert l % CHUNK == 0 and l % GRID_W == 0
    tm = _tile(l, 512)
    cos, sin = _rope_tables(l)
    z, xc, dt, qp, kp, vt = _in_proj(x, w["g_pre"], w["wxbc"], w["wrest"], cos, sin, w["qg"], w["kg"],
                                     w["conv_w"], w["conv_b"], tm)
    yf, yb = _ssd(xc, dt, w["dt_bias"], w["a_log"], w["d_skip"], _tile(l, 8 * CHUNK) // CHUNK)
    o = _attention(w["attn_bounded"], qp, kp, vt, _tile(l, 1024), _tile(l, 2048))
    kx, vx = _mem_kv(mem, w["g_mem"], w["wxk"], w["wxv"])
    x2 = _mix_xattn(x, yf, yb, z, o, kx, vx, w["g_ssm"], w["g_att"], w["wout"], w["g_post"],
                    w["g_xpre"], w["wxq"], w["wxo"], w["g_xpost"], _tile(l, 1024))
    return _ffn(x2, w["g_fpre"], w["wg"], w["wu"], w["wd"], w["g_fpost"], tm)


def _prep_weights(i, norm_mix_pre, w_in, conv_w, conv_b, dt_bias, a_log, d_skip, ssm_norm, q_norm, k_norm,
                  attn_norm, w_out, norm_mix_post, norm_x_pre, norm_mem, w_xq, w_xk, w_xv, w_xo, norm_x_post,
                  norm_ffn_pre, w_gate, w_up, w_down, norm_ffn_post):
    row = lambda v: v.reshape(1, -1).astype(F32)
    splits = np.cumsum([SSM_INNER, CONV_DIM, 2 * SSM_HEADS, ATTN_INNER,
                        ATTN_KV_HEADS * ATTN_HEAD_DIM, ATTN_KV_HEADS * ATTN_HEAD_DIM])[:-1]
    wz, wxbc, wdt, wq, wk, wv = jnp.split(w_in[i], [int(s) for s in splits], axis=-1)
    order = [h + ATTN_GROUP * g for h in range(ATTN_GROUP) for g in range(ATTN_KV_HEADS)]
    wq = wq.reshape(D_MODEL, ATTN_HEADS, ATTN_HEAD_DIM)[:, order, :].reshape(D_MODEL, ATTN_INNER)
    pad = DT_PAD - 2 * SSM_HEADS
    q_gain = row(q_norm[i]) * (ATTN_HEAD_DIM ** -0.5 * np.log2(np.e))
    k_gain = row(k_norm[i])
    score_bound = ATTN_HEAD_DIM * jnp.max(jnp.abs(q_gain)) * jnp.max(jnp.abs(k_gain))
    return {
        "attn_bounded": (score_bound * BOUND_MARGIN <= MAX_EXP2_SCORE).astype(jnp.int32).reshape(1),
        "g_pre": row(norm_mix_pre[i]),
        "wxbc": wxbc.astype(BF16),
        "wrest": jnp.concatenate([wz, wq, wk, wv, jnp.pad(wdt, ((0, 0), (0, pad)))], axis=1).astype(BF16),
        "qg": jnp.tile(q_gain, (1, LANES // ATTN_HEAD_DIM)),
        "kg": jnp.tile(k_gain, (1, LANES // ATTN_HEAD_DIM)),
        "conv_w": conv_w[i].astype(F32), "conv_b": row(conv_b[i]),
        "dt_bias": jnp.pad(row(dt_bias[i]), ((0, 0), (0, pad))),
        "a_log": jnp.pad(row(a_log[i]), ((0, 0), (0, pad))),
        "d_skip": jnp.repeat(row(d_skip[i]), SSM_HEAD_DIM, axis=1),
        "g_ssm": row(ssm_norm[i]), "g_att": row(attn_norm[i]),
        "wout": w_out[i].astype(BF16),
        "g_post": row(norm_mix_post[i]), "g_xpre": row(norm_x_pre[i]), "g_mem": row(norm_mem[i]),
        "wxq": w_xq[i].astype(BF16), "wxk": w_xk[i].astype(BF16), "wxv": w_xv[i].astype(BF16),
        "wxo": w_xo[i].astype(BF16), "g_xpost": row(norm_x_post[i]),
        "g_fpre": row(norm_ffn_pre[i]), "wg": w_gate[i].astype(BF16), "wu": w_up[i].astype(BF16),
        "wd": w_down[i].astype(BF16), "g_fpost": row(norm_ffn_post[i]),
    }


def kernel(x_prompt, x_sample, mem_prompt, mem_sample, norm_mix_pre, w_in, conv_w, conv_b, dt_bias, a_log,
           d_skip, ssm_norm, q_norm, k_norm, attn_norm, w_out, norm_mix_post, norm_x_pre, norm_mem, w_xq,
           w_xk, w_xv, w_xo, norm_x_post, norm_ffn_pre, w_gate, w_up, w_down, norm_ffn_post):
    y_prompt, y_sample = x_prompt, x_sample
    for i in range(w_in.shape[0]):
        w = _prep_weights(i, norm_mix_pre, w_in, conv_w, conv_b, dt_bias, a_log, d_skip, ssm_norm, q_norm,
                          k_norm, attn_norm, w_out, norm_mix_post, norm_x_pre, norm_mem, w_xq, w_xk, w_xv,
                          w_xo, norm_x_post, norm_ffn_pre, w_gate, w_up, w_down, norm_ffn_post)
        y_prompt = _layer(y_prompt, mem_prompt, w)
        y_sample = _layer(y_sample, mem_sample, w)
    return (y_prompt, y_sample)
```

```python
import functools

import numpy as np
import jax
import jax.numpy as jnp
from jax import lax
from jax.experimental import pallas as pl
from jax.experimental.pallas import tpu as pltpu

F32 = jnp.float32
BF16 = jnp.bfloat16

EPS = 1e-6
LANES = 128
SUBLANES = 8
VMEM_LIMIT_BYTES = 56 * 1024 * 1024

D_MODEL = 1024
SSM_HEADS = 16
SSM_HEAD_DIM = 64
SSM_INNER = SSM_HEADS * SSM_HEAD_DIM
SSM_GROUPS = 2
SSM_STATE = 128
D_CONV = 5
CHUNK = 128
CONV_DIM = SSM_INNER + 2 * SSM_GROUPS * SSM_STATE
ATTN_HEADS = 8
ATTN_KV_HEADS = 2
ATTN_HEAD_DIM = 64
ATTN_INNER = ATTN_HEADS * ATTN_HEAD_DIM
ATTN_GROUP = ATTN_HEADS // ATTN_KV_HEADS
GRID_W = 64
ROPE_THETA = 10000.0
X_HEADS = 4
X_HEAD_DIM = 128
X_INNER = X_HEADS * X_HEAD_DIM
DT_PAD = LANES
HALO = SUBLANES
MAX_EXP2_SCORE = 60.0
BOUND_MARGIN = 1.02
Q_STRIP = 512


def _dot(a, b):
    return jnp.dot(a, b, preferred_element_type=F32)


def _dot_nt(a, b):
    return lax.dot_general(a, b, (((1,), (1,)), ((), ())), preferred_element_type=F32)


def _dot_tn(a, b):
    return lax.dot_general(a, b, (((0,), (0,)), ((), ())), preferred_element_type=F32)


def _rmsnorm(x, g):
    return x * lax.rsqrt(jnp.mean(x * x, axis=-1, keepdims=True) + EPS) * g


LOG2E = float(np.log2(np.e))


def _sigmoid(x):
    return 1.0 / (1.0 + jnp.exp2(x * -LOG2E))


def _params(*sem):
    return pltpu.CompilerParams(dimension_semantics=sem, vmem_limit_bytes=VMEM_LIMIT_BYTES)


def _const_spec(shape):
    nd = len(shape)
    return pl.BlockSpec(shape, lambda *_: (0,) * nd, pipeline_mode=pl.Buffered(1))


def _in_proj_kernel(x_ref, xp_ref, xn_ref, g_ref, wxbc_ref, wrest_ref,
                    cos_ref, sin_ref, qg_ref, kg_ref, cw_ref, cb_ref,
                    z_ref, xc_ref, dt_ref, qp_ref, kp_ref, vt_ref, ext_ref, conv_ref):
    i = pl.program_id(1)
    n = pl.num_programs(1)
    tm = x_ref.shape[1]
    x_all = jnp.concatenate([x_ref[0], xp_ref[0], xn_ref[0]], axis=0)
    h_all = _rmsnorm(x_all, g_ref[...]).astype(BF16)
    h = h_all[:tm]

    xbc = _dot(h_all, wxbc_ref[...])
    for s in range(CONV_DIM // LANES):
        cols = slice(s * LANES, (s + 1) * LANES)
        ext_ref[s, 0:HALO, :] = jnp.where(i > 0, xbc[tm:tm + HALO, cols], 0.0)
        ext_ref[s, HALO:HALO + tm, :] = xbc[:tm, cols]
        ext_ref[s, HALO + tm:HALO + tm + HALO, :] = jnp.where(i < n - 1, xbc[tm + HALO:, cols], 0.0)
        ext_ref[s, HALO + tm + HALO:, :] = jnp.zeros((HALO, LANES), F32)
    pitch = tm // SUBLANES + 1
    for s in range(CONV_DIM // LANES):
        cols = slice(s * LANES, (s + 1) * LANES)
        taps = [cw_ref[j:j + 1, cols] for j in range(D_CONV)]
        bias = cb_ref[:, cols]
        for a in range(pitch):
            acc = bias
            for j in range(D_CONV):
                start = HALO - D_CONV // 2 + j + a
                acc = acc + ext_ref[s, pl.ds(start, SUBLANES, stride=pitch), :] * taps[j]
            conv_ref[s, pl.ds(a, SUBLANES, stride=pitch), :] = acc * _sigmoid(acc)
        xc_ref[0, :, cols] = conv_ref[s, 0:tm, :].astype(BF16)

    rest = _dot(h, wrest_ref[...])
    z_ref[0] = rest[:, :SSM_INNER].astype(BF16)
    q = rest[:, SSM_INNER:SSM_INNER + ATTN_INNER]
    k = rest[:, SSM_INNER + ATTN_INNER:SSM_INNER + ATTN_INNER + LANES]
    v = rest[:, SSM_INNER + ATTN_INNER + LANES:SSM_INNER + ATTN_INNER + 2 * LANES]
    dt_ref[0] = rest[:, SSM_INNER + ATTN_INNER + 2 * LANES:]
    vt_ref[0] = v.T.astype(BF16)

    cos = cos_ref[...]
    sin = sin_ref[...]
    lane = lax.broadcasted_iota(jnp.int32, cos.shape, 1)

    r_id = (lax.broadcasted_iota(jnp.int32, (2 * LANES, LANES), 0) % LANES) // ATTN_HEAD_DIM
    c_id = lax.broadcasted_iota(jnp.int32, (2 * LANES, LANES), 1) // ATTN_HEAD_DIM
    same_head = jnp.where(r_id == c_id, 1.0, 0.0).astype(BF16)

    def head_norm_rope(blk, gain):
        sq = blk * blk
        hi = sq.astype(BF16)
        lo = (sq - hi.astype(F32)).astype(BF16)
        ss = _dot(jnp.concatenate([hi, lo], axis=1), same_head)
        y = blk * lax.rsqrt(ss * (1.0 / ATTN_HEAD_DIM) + EPS) * gain
        partner = jnp.where((lane & 16) != 0, pltpu.roll(y, 16, 1), pltpu.roll(y, LANES - 16, 1))
        return y * cos + partner * sin

    kp_ref[0] = head_norm_rope(k, kg_ref[...]).astype(BF16)
    qg = qg_ref[...]
    zeros = jnp.zeros((ATTN_HEAD_DIM, tm), BF16)
    for j in range(ATTN_GROUP):
        r_t = head_norm_rope(q[:, j * LANES:(j + 1) * LANES], qg).T.astype(BF16)
        qp_ref[0, j * LANES:(j + 1) * LANES, :] = jnp.concatenate([r_t[:ATTN_HEAD_DIM], zeros], axis=0)
        qp_ref[0, (ATTN_GROUP + j) * LANES:(ATTN_GROUP + j + 1) * LANES, :] = jnp.concatenate(
            [zeros, r_t[ATTN_HEAD_DIM:]], axis=0)


def _in_proj(x, g, wxbc, wrest, cos, sin, qg, kg, cw, cb, tm):
    b, l, d = x.shape
    grid = (b, l // tm)
    nh = tm // HALO
    last = l // HALO - 1
    tok = lambda n: pl.BlockSpec((1, tm, n), lambda bi, i: (bi, i, 0))
    pos = pl.BlockSpec((tm, LANES), lambda bi, i: (i, 0))
    prev_rows = pl.BlockSpec((1, HALO, d), lambda bi, i: (bi, jnp.maximum(i * nh - 1, 0), 0))
    next_rows = pl.BlockSpec((1, HALO, d), lambda bi, i: (bi, jnp.minimum((i + 1) * nh, last), 0))
    out_shape = (
        jax.ShapeDtypeStruct((b, l, SSM_INNER), BF16),
        jax.ShapeDtypeStruct((b, l, CONV_DIM), BF16),
        jax.ShapeDtypeStruct((b, l, DT_PAD), F32),
        jax.ShapeDtypeStruct((b, ATTN_HEADS * LANES, l), BF16),
        jax.ShapeDtypeStruct((b, l, LANES), BF16),
        jax.ShapeDtypeStruct((b, LANES, l), BF16),
    )
    tok_t = lambda n: pl.BlockSpec((1, n, tm), lambda bi, i: (bi, 0, i))
    return pl.pallas_call(
        _in_proj_kernel,
        grid=grid,
        in_specs=[tok(d), prev_rows, next_rows, _const_spec(g.shape), _const_spec(wxbc.shape),
                  _const_spec(wrest.shape), pos, pos, _const_spec(qg.shape), _const_spec(kg.shape),
                  _const_spec(cw.shape), _const_spec(cb.shape)],
        out_specs=(tok(SSM_INNER), tok(CONV_DIM), tok(DT_PAD), tok_t(ATTN_HEADS * LANES), tok(LANES),
                   tok_t(LANES)),
        out_shape=out_shape,
        scratch_shapes=[pltpu.VMEM((CONV_DIM // LANES, tm + 3 * HALO, LANES), F32),
                        pltpu.VMEM((CONV_DIM // LANES, tm + SUBLANES, LANES), F32)],
        compiler_params=_params("parallel", "parallel"),
        name="in_proj",
    )(x, x, x, g, wxbc, wrest, cos, sin, qg, kg, cw, cb)


def _expand_heads(w, sel):
    hi = w.astype(BF16)
    lo = (w - hi.astype(F32)).astype(BF16)
    return _dot(jnp.concatenate([hi, lo], axis=1), sel)


def _chunk_mask(reverse):
    row = lax.broadcasted_iota(jnp.int32, (CHUNK, CHUNK), 0)
    col = lax.broadcasted_iota(jnp.int32, (CHUNK, CHUNK), 1)
    return (row <= col) if reverse else (row >= col)


def _ssd_decays(dt_raw, dtb, a_neg, sel, reverse):
    n_chunks = dt_raw.shape[0] // CHUNK
    tri = _chunk_mask(reverse).astype(F32)
    end = 0 if reverse else CHUNK - 1
    xpre = dt_raw + dtb
    dtv = jnp.maximum(xpre, 0.0) + jnp.log1p(jnp.exp(-jnp.abs(xpre)))
    a = dtv * a_neg
    cs_chunks = [jnp.dot(tri, a[c * CHUNK:(c + 1) * CHUNK], preferred_element_type=F32,
                         precision=lax.Precision.HIGHEST) for c in range(n_chunks)]
    cs = jnp.concatenate(cs_chunks, axis=0)
    cs_end = jnp.concatenate([jnp.broadcast_to(c[end:end + 1, :], (CHUNK, LANES)) for c in cs_chunks], axis=0)
    e3 = _expand_heads(jnp.exp(cs), sel)
    e2 = _expand_heads(dtv * jnp.exp(cs_end - cs), sel)
    src_t = [(c - jnp.log(dtv[i * CHUNK:(i + 1) * CHUNK])).T for i, c in enumerate(cs_chunks)]
    return cs_chunks, src_t, e2, e3


def _ssd_chunk(xc, cs, src_t, e2, e3, h_ref, reverse):
    off = SSM_HEADS if reverse else 0
    mask = _chunk_mask(reverse)
    end = 0 if reverse else CHUNK - 1
    xs = xc[:, :SSM_INNER]
    xdec = (xs.astype(F32) * e2).astype(BF16)
    lane = lax.broadcasted_iota(jnp.int32, (CHUNK, LANES), 1)
    low = lane < SSM_HEAD_DIM
    zero = jnp.zeros((CHUNK, LANES), BF16)
    gw = SSM_INNER // SSM_GROUPS
    hpg = SSM_HEADS // SSM_GROUPS
    y_tiles = []
    for g in range(SSM_GROUPS):
        bm = xc[:, SSM_INNER + g * SSM_STATE:SSM_INNER + (g + 1) * SSM_STATE]
        cm = xc[:, SSM_INNER + (SSM_GROUPS + g) * SSM_STATE:SSM_INNER + (SSM_GROUPS + g + 1) * SSM_STATE]
        cb = _dot_nt(cm, bm)
        h_prev = h_ref[:, g * gw:(g + 1) * gw]
        y_off = _dot(cm, h_prev.astype(BF16)) * e3[:, g * gw:(g + 1) * gw]
        for jj in range(hpg // 2):
            j = g * (hpg // 2) + jj
            x_tile = xs[:, j * LANES:(j + 1) * LANES]
            ms = []
            for half in range(2):
                hd = off + 2 * j + half
                seg = cs[:, hd:hd + 1] - src_t[hd:hd + 1, :]
                ms.append((cb * jnp.exp(jnp.where(mask, seg, -jnp.inf))).astype(BF16))
            x_pair = jnp.concatenate([jnp.where(low, x_tile, zero), jnp.where(low, zero, x_tile)], axis=0)
            y_tiles.append(y_off[:, jj * LANES:(jj + 1) * LANES] + _dot(jnp.concatenate(ms, axis=1), x_pair))
        s_new = _dot_tn(bm, xdec[:, g * gw:(g + 1) * gw])
        h_ref[:, g * gw:(g + 1) * gw] = h_prev * e3[end:end + 1, g * gw:(g + 1) * gw] + s_new
    return jnp.concatenate(y_tiles, axis=1)


def _ssd_kernel(xf_ref, xb_ref, dtf_ref, dtb_ref, bias_ref, alog_ref, dskip_ref, sel_ref,
                yf_ref, yb_ref, hf_ref, hb_ref):
    c = pl.program_id(1)

    @pl.when(c == 0)
    def _():
        hf_ref[...] = jnp.zeros_like(hf_ref)
        hb_ref[...] = jnp.zeros_like(hb_ref)

    a_neg = -jnp.exp(alog_ref[...])
    bias = bias_ref[...]
    n_chunks = xf_ref.shape[1] // CHUNK
    dec_f = _ssd_decays(dtf_ref[0], bias, a_neg, sel_ref[0], reverse=False)
    dec_b = _ssd_decays(dtb_ref[0], bias, a_neg, sel_ref[1], reverse=True)

    def chunk(x_ref, dec, h_ref, c, reverse):
        cs, src_t, e2, e3 = dec
        rows = slice(c * CHUNK, (c + 1) * CHUNK)
        xc = x_ref[0, rows, :]
        return xc, _ssd_chunk(xc, cs[c], src_t[c], e2[rows], e3[rows], h_ref, reverse)

    for c in range(n_chunks):
        xf, yf = chunk(xf_ref, dec_f, hf_ref, c, False)
        yf_ref[0, c * CHUNK:(c + 1) * CHUNK, :] = (
            yf + xf[:, :SSM_INNER].astype(F32) * dskip_ref[...]).astype(BF16)
        cb = n_chunks - 1 - c
        _, yb = chunk(xb_ref, dec_b, hb_ref, cb, True)
        yb_ref[0, cb * CHUNK:(cb + 1) * CHUNK, :] = yb.astype(BF16)


def _ssd(xc, dt, bias, alog, dskip, n_chunks):
    b, l, _ = xc.shape
    rows = n_chunks * CHUNK
    nc = l // rows
    fwd = lambda n: pl.BlockSpec((1, rows, n), lambda bi, c: (bi, c, 0))
    bwd = lambda n: pl.BlockSpec((1, rows, n), lambda bi, c: (bi, nc - 1 - c, 0))
    lane_head = np.arange(SSM_INNER) // SSM_HEAD_DIM
    sel = np.stack([np.arange(DT_PAD)[:, None] == (d * SSM_HEADS + lane_head)[None, :] for d in range(2)])
    sel = jnp.asarray(np.concatenate([sel, sel], axis=1), BF16)
    return pl.pallas_call(
        _ssd_kernel,
        grid=(b, nc),
        in_specs=[fwd(CONV_DIM), bwd(CONV_DIM), fwd(DT_PAD), bwd(DT_PAD),
                  _const_spec(bias.shape), _const_spec(alog.shape), _const_spec(dskip.shape),
                  _const_spec(sel.shape)],
        out_specs=(fwd(SSM_INNER), bwd(SSM_INNER)),
        out_shape=(jax.ShapeDtypeStruct((b, l, SSM_INNER), BF16),
                   jax.ShapeDtypeStruct((b, l, SSM_INNER), BF16)),
        scratch_shapes=[pltpu.VMEM((SSM_STATE, SSM_INNER), F32),
                        pltpu.VMEM((SSM_STATE, SSM_INNER), F32)],
        compiler_params=_params("parallel", "arbitrary"),
        name="ssd",
    )(xc, xc, dt, dt, bias, alog, dskip, sel)


def _attn_kernel(bounded_ref, q_ref, k_ref, vt_ref, o_ref, m_ref, l_ref, acc_ref, *, tk):
    nk = k_ref.shape[1] // tk
    l_ref[...] = jnp.zeros_like(l_ref)
    acc_ref[...] = jnp.zeros_like(acc_ref)

    def tiles(i):
        ks = pl.multiple_of(i * tk, tk)
        return k_ref[0, pl.ds(ks, tk), :], vt_ref[0, :, pl.ds(ks, tk)]

    tq = q_ref.shape[2]
    strip = min(tq, Q_STRIP)

    def bounded_body(i, carry):
        k, vt = tiles(i)
        for j in range(ATTN_GROUP):
            rows = slice(j * ATTN_HEAD_DIM, (j + 1) * ATTN_HEAD_DIM)
            for c in range(tq // strip):
                cols = slice(c * strip, (c + 1) * strip)
                p_t = jnp.exp2(_dot(k, q_ref[0, j * LANES:(j + 1) * LANES, cols]))
                part = p_t[0:SUBLANES, :]
                for r in range(1, tk // SUBLANES):
                    part = part + p_t[r * SUBLANES:(r + 1) * SUBLANES, :]
                l_ref[j * SUBLANES:(j + 1) * SUBLANES, cols] += part
                acc_ref[rows, cols] += _dot(vt, p_t.astype(BF16))
        return carry

    def online_body(i, carry):
        k, vt = tiles(i)
        for j in range(ATTN_GROUP):
            rows = slice(j * ATTN_HEAD_DIM, (j + 1) * ATTN_HEAD_DIM)
            s_t = _dot(k, q_ref[0, j * LANES:(j + 1) * LANES, :])
            m_prev = m_ref[j:j + 1, :]
            m_new = jnp.maximum(m_prev, jnp.max(s_t, axis=0, keepdims=True))
            alpha = jnp.exp2(m_prev - m_new)
            p_t = jnp.exp2(s_t - m_new)
            lrow = slice(j * SUBLANES, j * SUBLANES + 1)
            l_ref[lrow, :] = alpha * l_ref[lrow, :] + jnp.sum(p_t, axis=0, keepdims=True)
            acc_ref[rows, :] = alpha * acc_ref[rows, :] + _dot(vt, p_t.astype(BF16))
            m_ref[j:j + 1, :] = m_new
        return carry

    bounded = bounded_ref[0] != 0

    @pl.when(bounded)
    def _():
        lax.fori_loop(0, nk, bounded_body, 0)

    @pl.when(jnp.logical_not(bounded))
    def _():
        m_ref[...] = jnp.full_like(m_ref, -jnp.inf)
        lax.fori_loop(0, nk, online_body, 0)

    for j in range(ATTN_GROUP):
        rows = slice(j * ATTN_HEAD_DIM, (j + 1) * ATTN_HEAD_DIM)
        l = jnp.sum(l_ref[j * SUBLANES:(j + 1) * SUBLANES, :], axis=0, keepdims=True)
        acc_ref[rows, :] = acc_ref[rows, :] / l
    o_ref[0] = acc_ref[...].T


def _attention(bounded, qp, kp, vt, tq, tk):
    b, _, l = qp.shape
    gw = ATTN_GROUP * LANES
    return pl.pallas_call(
        functools.partial(_attn_kernel, tk=tk),
        grid=(b, ATTN_KV_HEADS, l // tq),
        in_specs=[pl.BlockSpec(memory_space=pltpu.SMEM),
                  pl.BlockSpec((1, gw, tq), lambda bi, g, i: (bi, g, i)),
                  pl.BlockSpec((1, l, LANES), lambda bi, g, i: (bi, 0, 0)),
                  pl.BlockSpec((1, ATTN_HEAD_DIM, l), lambda bi, g, i: (bi, g, 0))],
        out_specs=pl.BlockSpec((1, tq, ATTN_GROUP * ATTN_HEAD_DIM), lambda bi, g, i: (bi, i, g)),
        out_shape=jax.ShapeDtypeStruct((b, l, ATTN_INNER), F32),
        scratch_shapes=[pltpu.VMEM((SUBLANES, tq), F32), pltpu.VMEM((ATTN_GROUP * SUBLANES, tq), F32),
                        pltpu.VMEM((ATTN_GROUP * ATTN_HEAD_DIM, tq), F32)],
        compiler_params=_params("parallel", "parallel", "parallel"),
        name="attention",
    )(bounded, qp, kp, vt)


def _mem_kv_kernel(mem_ref, g_ref, wk_ref, wv_ref, k_ref, v_ref):
    m = _rmsnorm(mem_ref[0], g_ref[...]).astype(BF16)
    k_ref[0] = _dot(m, wk_ref[...]).astype(BF16)
    v_ref[0] = _dot(m, wv_ref[...]).astype(BF16)


def _mem_kv(mem, g, wk, wv):
    b, n, d = mem.shape
    spec = pl.BlockSpec((1, n, X_INNER), lambda bi: (bi, 0, 0))
    return pl.pallas_call(
        _mem_kv_kernel,
        grid=(b,),
        in_specs=[pl.BlockSpec((1, n, d), lambda bi: (bi, 0, 0)), _const_spec(g.shape),
                  _const_spec(wk.shape), _const_spec(wv.shape)],
        out_specs=(spec, spec),
        out_shape=(jax.ShapeDtypeStruct((b, n, X_INNER), BF16),) * 2,
        compiler_params=_params("parallel"),
        name="mem_kv",
    )(mem, g, wk, wv)


def _mix_xattn_kernel(x_ref, yf_ref, yb_ref, z_ref, o_ref, kx_ref, vx_ref,
                      gssm_ref, gatt_ref, wout_ref, gpost_ref,
                      gxpre_ref, wxq_ref, wxo_ref, gxpost_ref, out_ref):
    tm = x_ref.shape[1]
    gw = SSM_INNER // SSM_GROUPS
    scale = X_HEAD_DIM ** -0.5

    def gate_norm(rows):
        z = z_ref[0, rows, :].astype(F32)
        y = (yf_ref[0, rows, :].astype(F32) + yb_ref[0, rows, :].astype(F32)) * (z * _sigmoid(z))
        yn = jnp.concatenate(
            [y[:, g * gw:(g + 1) * gw]
             * lax.rsqrt(jnp.mean(y[:, g * gw:(g + 1) * gw] * y[:, g * gw:(g + 1) * gw], axis=-1, keepdims=True)
                         + EPS)
             for g in range(SSM_GROUPS)], axis=1)
        return (yn * gssm_ref[...]).astype(BF16), _rmsnorm(o_ref[0, rows, :], gatt_ref[...]).astype(BF16)

    def mix_proj(ys):
        return _dot(ys[0], wout_ref[:SSM_INNER, :]) + _dot(ys[1], wout_ref[SSM_INNER:, :])

    def residual_prenorm(rows, mix):
        x1 = x_ref[0, rows, :] + _rmsnorm(mix, gpost_ref[...])
        return x1, _rmsnorm(x1, gxpre_ref[...]).astype(BF16)

    def attend(h):
        q = _dot(h, wxq_ref[...])
        heads = []
        for hh in range(X_HEADS):
            cols = slice(hh * X_HEAD_DIM, (hh + 1) * X_HEAD_DIM)
            s = _dot_nt(q[:, cols].astype(BF16), kx_ref[0, :, cols])
            e = jnp.exp2((s - jnp.max(s, axis=-1, keepdims=True)) * (scale * LOG2E))
            inv = 1.0 / jnp.sum(e, axis=-1, keepdims=True)
            heads.append(_dot(e.astype(BF16), vx_ref[0, :, cols]) * inv)
        return _dot(jnp.concatenate(heads, axis=1).astype(BF16), wxo_ref[...])

    halves = [slice(r * (tm // 2), (r + 1) * (tm // 2)) for r in range(2)]
    ys = [gate_norm(rows) for rows in halves]
    mixes = [mix_proj(y) for y in ys]
    xh = [residual_prenorm(rows, m) for rows, m in zip(halves, mixes)]
    cas = [attend(h) for _, h in xh]
    for rows, (x1, _), ca in zip(halves, xh, cas):
        out_ref[0, rows, :] = x1 + _rmsnorm(ca, gxpost_ref[...])


def _mix_xattn(x, yf, yb, z, o, kx, vx, gssm, gatt, wout, gpost, gxpre, wxq, wxo, gxpost, tm):
    b, l, d = x.shape
    n_mem = kx.shape[1]
    tok = lambda n: pl.BlockSpec((1, tm, n), lambda bi, i: (bi, i, 0))
    mem = pl.BlockSpec((1, n_mem, X_INNER), lambda bi, i: (bi, 0, 0))
    consts = (gssm, gatt, wout, gpost, gxpre, wxq, wxo, gxpost)
    return pl.pallas_call(
        _mix_xattn_kernel,
        grid=(b, l // tm),
        in_specs=[tok(d), tok(SSM_INNER), tok(SSM_INNER), tok(SSM_INNER), tok(ATTN_INNER), mem, mem]
                 + [_const_spec(c.shape) for c in consts],
        out_specs=tok(d),
        out_shape=jax.ShapeDtypeStruct((b, l, d), F32),
        compiler_params=_params("parallel", "parallel"),
        name="mix_xattn",
    )(x, yf, yb, z, o, kx, vx, *consts)


def _ffn_kernel(x_ref, gpre_ref, wg_ref, wu_ref, wd_ref, gpost_ref, out_ref):
    tm = x_ref.shape[1]
    halves = [slice(r * (tm // 2), (r + 1) * (tm // 2)) for r in range(2)]
    xs = [x_ref[0, rows, :] for rows in halves]
    hs = [_rmsnorm(x, gpre_ref[...]).astype(BF16) for x in xs]
    gu = [(_dot(h, wg_ref[...]), _dot(h, wu_ref[...])) for h in hs]
    acts = [(gt * _sigmoid(gt) * up).astype(BF16) for gt, up in gu]
    fs = [_dot(a, wd_ref[...]) for a in acts]
    for rows, x, f in zip(halves, xs, fs):
        out_ref[0, rows, :] = x + _rmsnorm(f, gpost_ref[...])


def _ffn(x, gpre, wg, wu, wd, gpost, tm):
    b, l, d = x.shape
    tok = pl.BlockSpec((1, tm, d), lambda bi, i: (bi, i, 0))
    single = lambda a: pl.BlockSpec(a.shape, lambda *_: (0,) * a.ndim, pipeline_mode=pl.Buffered(1))
    return pl.pallas_call(
        _ffn_kernel,
        grid=(b, l // tm),
        in_specs=[tok, _const_spec(gpre.shape), single(wg), single(wu), single(wd), _const_spec(gpost.shape)],
        out_specs=tok,
        out_shape=jax.ShapeDtypeStruct((b, l, d), F32),
        compiler_params=_params("parallel", "parallel"),
        name="ffn",
    )(x, gpre, wg, wu, wd, gpost)


def _rope_tables(l):
    t = jnp.arange(l, dtype=jnp.int32)
    r = (t // GRID_W).astype(F32)
    c = (t % GRID_W).astype(F32)
    half = ATTN_HEAD_DIM // 2
    inv = 1.0 / (ROPE_THETA ** (jnp.arange(0, half, 2, dtype=F32) / half))
    ang_r = r[:, None] * inv
    ang_c = c[:, None] * inv
    cos = jnp.concatenate([jnp.cos(ang_r)] * 2 + [jnp.cos(ang_c)] * 2, axis=1)
    sin = jnp.concatenate([-jnp.sin(ang_r), jnp.sin(ang_r), -jnp.sin(ang_c), jnp.sin(ang_c)], axis=1)
    return jnp.tile(cos, (1, 2)), jnp.tile(sin, (1, 2))


def _tile(l, want):
    t = min(l, want)
    assert l % t == 0
    return t


def _layer(x, mem, w):
    b, l, _ = x.shape
    assert l % CHUNK == 0 and l % GRID_W == 0
    tm = _tile(l, 512)
    cos, sin = _rope_tables(l)
    z, xc, dt, qp, kp, vt = _in_proj(x, w["g_pre"], w["wxbc"], w["wrest"], cos, sin, w["qg"], w["kg"],
                                     w["conv_w"], w["conv_b"], tm)
    yf, yb = _ssd(xc, dt, w["dt_bias"], w["a_log"], w["d_skip"], _tile(l, 8 * CHUNK) // CHUNK)
    o = _attention(w["attn_bounded"], qp, kp, vt, _tile(l, 1024), _tile(l, 2048))
    kx, vx = _mem_kv(mem, w["g_mem"], w["wxk"], w["wxv"])
    x2 = _mix_xattn(x, yf, yb, z, o, kx, vx, w["g_ssm"], w["g_att"], w["wout"], w["g_post"],
                    w["g_xpre"], w["wxq"], w["wxo"], w["g_xpost"], _tile(l, 1024))
    return _ffn(x2, w["g_fpre"], w["wg"], w["wu"], w["wd"], w["g_fpost"], tm)


def _prep_weights(i, norm_mix_pre, w_in, conv_w, conv_b, dt_bias, a_log, d_skip, ssm_norm, q_norm, k_norm,
                  attn_norm, w_out, norm_mix_post, norm_x_pre, norm_mem, w_xq, w_xk, w_xv, w_xo, norm_x_post,
                  norm_ffn_pre, w_gate, w_up, w_down, norm_ffn_post):
    row = lambda v: v.reshape(1, -1).astype(F32)
    splits = np.cumsum([SSM_INNER, CONV_DIM, 2 * SSM_HEADS, ATTN_INNER,
                        ATTN_KV_HEADS * ATTN_HEAD_DIM, ATTN_KV_HEADS * ATTN_HEAD_DIM])[:-1]
    wz, wxbc, wdt, wq, wk, wv = jnp.split(w_in[i], [int(s) for s in splits], axis=-1)
    order = [h + ATTN_GROUP * g for h in range(ATTN_GROUP) for g in range(ATTN_KV_HEADS)]
    wq = wq.reshape(D_MODEL, ATTN_HEADS, ATTN_HEAD_DIM)[:, order, :].reshape(D_MODEL, ATTN_INNER)
    pad = DT_PAD - 2 * SSM_HEADS
    q_gain = row(q_norm[i]) * (ATTN_HEAD_DIM ** -0.5 * np.log2(np.e))
    k_gain = row(k_norm[i])
    score_bound = ATTN_HEAD_DIM * jnp.max(jnp.abs(q_gain)) * jnp.max(jnp.abs(k_gain))
    return {
        "attn_bounded": (score_bound * BOUND_MARGIN <= MAX_EXP2_SCORE).astype(jnp.int32).reshape(1),
        "g_pre": row(norm_mix_pre[i]),
        "wxbc": wxbc.astype(BF16),
        "wrest": jnp.concatenate([wz, wq, wk, wv, jnp.pad(wdt, ((0, 0), (0, pad)))], axis=1).astype(BF16),
        "qg": jnp.tile(q_gain, (1, LANES // ATTN_HEAD_DIM)),
        "kg": jnp.tile(k_gain, (1, LANES // ATTN_HEAD_DIM)),
        "conv_w": conv_w[i].astype(F32), "conv_b": row(conv_b[i]),
        "dt_bias": jnp.pad(row(dt_bias[i]), ((0, 0), (0, pad))),
        "a_log": jnp.pad(row(a_log[i]), ((0, 0), (0, pad))),
        "d_skip": jnp.repeat(row(d_skip[i]), SSM_HEAD_DIM, axis=1),
        "g_ssm": row(ssm_norm[i]), "g_att": row(attn_norm[i]),
        "wout": w_out[i].astype(BF16),
        "g_post": row(norm_mix_post[i]), "g_xpre": row(norm_x_pre[i]), "g_mem": row(norm_mem[i]),
        "wxq": w_xq[i].astype(BF16), "wxk": w_xk[i].astype(BF16), "wxv": w_xv[i].astype(BF16),
        "wxo": w_xo[i].astype(BF16), "g_xpost": row(norm_x_post[i]),
        "g_fpre": row(norm_ffn_pre[i]), "wg": w_gate[i].astype(BF16), "wu": w_up[i].astype(BF16),
        "wd": w_down[i].astype(BF16), "g_fpost": row(norm_ffn_post[i]),
    }


def kernel(x_prompt, x_sample, mem_prompt, mem_sample, norm_mix_pre, w_in, conv_w, conv_b, dt_bias, a_log,
           d_skip, ssm_norm, q_norm, k_norm, attn_norm, w_out, norm_mix_post, norm_x_pre, norm_mem, w_xq,
           w_xk, w_xv, w_xo, norm_x_post, norm_ffn_pre, w_gate, w_up, w_down, norm_ffn_post):
    y_prompt, y_sample = x_prompt, x_sample
    for i in range(w_in.shape[0]):
        w = _prep_weights(i, norm_mix_pre, w_in, conv_w, conv_b, dt_bias, a_log, d_skip, ssm_norm, q_norm,
                          k_norm, attn_norm, w_out, norm_mix_post, norm_x_pre, norm_mem, w_xq, w_xk, w_xv,
                          w_xo, norm_x_post, norm_ffn_pre, w_gate, w_up, w_down, norm_ffn_post)
        y_prompt = _layer(y_prompt, mem_prompt, w)
        y_sample = _layer(y_sample, mem_sample, w)
    return (y_prompt, y_sample)
```
